```python
import jax
import jax.numpy as jnp
from jax import lax
import numpy as np

D_MODEL = 2048
BATCH = 2
SEQ = 4096
DEPTH = 1
DEC_BATCH = 32
DEC_SEQ = 4
PAST_LEN = 8192
PAGE_SIZE = 128

HEAD_DIM = 128
NSA_HEADS = 8
NSA_GROUPS = 2
NSA_HPG = NSA_HEADS // NSA_GROUPS
CMP_LEN = 32
CMP_STRIDE = 16
CMP_RATIO = CMP_LEN // CMP_STRIDE
SEL_BLOCK = 64
N_SEL = 16
WINDOW = 512
SB_HEADS = 4
MEM_HEADS = 4
MEM_TOKENS = 256
D_FF = 5632
CONV_W = 3
Q_BLOCK = 128
ALPHA = (2.0 * DEPTH) ** 0.25
BETA = (8.0 * DEPTH) ** -0.25

NSA_Q = NSA_HEADS * HEAD_DIM
KVG = NSA_GROUPS * HEAD_DIM
SB_W = SB_HEADS * HEAD_DIM
MEM_W = MEM_HEADS * HEAD_DIM
KV_CH = 4 * KVG + 2 * SB_W
WIN_CH = 2 * KVG
MEM_KV_CH = 2 * MEM_W
OFF_KV = NSA_Q
OFF_WIN = OFF_KV + KV_CH
OFF_QSB = OFF_WIN + WIN_CH
OFF_QMEM = OFF_QSB + SB_W
OFF_GNSA = OFF_QMEM + MEM_W
OFF_GMERGE = OFF_GNSA + 3 * NSA_HEADS
PROJ_W = OFF_GMERGE + 3 * D_MODEL

kernel_name = 'nsa_stickbreak_memory_deepnorm_decoder_step'


def _layer_norm(x, g, b, eps=1e-5):
    xf = x.astype(jnp.float32)
    mu = jnp.mean(xf, axis=-1, keepdims=True)
    var = jnp.mean(jnp.square(xf - mu), axis=-1, keepdims=True)
    return ((xf - mu) * lax.rsqrt(var + eps) * g + b).astype(x.dtype)


def _masked_softmax(s, mask):
    s = jnp.where(mask, s, -1e30)
    m = jnp.max(s, axis=-1, keepdims=True)
    e = jnp.where(mask, jnp.exp(s - m), 0.0)
    return e / jnp.maximum(jnp.sum(e, axis=-1, keepdims=True), 1e-30)


def _alibi_slopes(n):
    return jnp.exp2(-8.0 * jnp.arange(1, n + 1, dtype=jnp.float32) / n)


def _query_blocks(t):
    qb = Q_BLOCK if t % Q_BLOCK == 0 else t
    return t // qb, qb


def _to_blocks(a, axis, nb):
    s = a.shape
    a = a.reshape(s[:axis] + (nb, s[axis] // nb) + s[axis + 1:])
    return jnp.moveaxis(a, axis, 0)


def _from_blocks(o):
    o = jnp.moveaxis(o, 0, 1)
    return o.reshape((o.shape[0], o.shape[1] * o.shape[2]) + o.shape[3:])


def _compress(k, w1, pe, w2):
    B, L, G, D = k.shape
    n_sub = L // CMP_STRIDE
    n_cmp = n_sub - CMP_RATIO + 1
    kb = k[:, :n_sub * CMP_STRIDE].reshape(B, n_sub, CMP_STRIDE, G, D)
    w1b = w1.reshape(CMP_RATIO, CMP_STRIDE, D, D)
    peb = pe.reshape(CMP_RATIO, CMP_STRIDE, D)
    hid = jnp.einsum('bnsgd,sde->bnge', kb[:, 0:n_cmp] + peb[0][:, None, :], w1b[0])
    for r in range(1, CMP_RATIO):
        hid = hid + jnp.einsum('bnsgd,sde->bnge', kb[:, r:r + n_cmp] + peb[r][:, None, :], w1b[r])
    return jax.nn.gelu(hid) @ w2


def _nsa(q, kv_all, win_all, gates, q_pos, win_pos, p):
    B, T = q.shape[:2]
    L = kv_all.shape[1]
    G, D = NSA_GROUPS, HEAD_DIM
    scale = D ** -0.5
    slopes = _alibi_slopes(NSA_HEADS).reshape(G, NSA_HPG)
    kc = kv_all[..., 0 * KVG:1 * KVG].reshape(B, L, G, D)
    vc = kv_all[..., 1 * KVG:2 * KVG].reshape(B, L, G, D)
    ks = kv_all[..., 2 * KVG:3 * KVG].reshape(B, L, G, D)
    vs = kv_all[..., 3 * KVG:4 * KVG].reshape(B, L, G, D)

    k_cmp = _compress(kc, p['w_cmp_k1'], p['pe_cmp_k'], p['w_cmp_k2'])
    v_cmp = _compress(vc, p['w_cmp_v1'], p['pe_cmp_v'], p['w_cmp_v2'])
    n_cmp = k_cmp.shape[1]
    cmp_start = jnp.arange(n_cmp) * CMP_STRIDE
    cmp_end = cmp_start + CMP_LEN - 1
    dist_c = q_pos[:, None] - cmp_end[None, :]
    s_c = (jnp.einsum('btghd,bngd->bghtn', q, k_cmp).astype(jnp.float32) * scale
           - slopes[:, :, None, None] * dist_c.astype(jnp.float32))
    p_cmp = _masked_softmax(s_c, dist_c >= 0)
    o_cmp = jnp.einsum('bghtn,bngd->btghd', p_cmp.astype(v_cmp.dtype), v_cmp)

    n_blk = -(-L // SEL_BLOCK)
    n_top = min(N_SEL, n_blk)
    blk = jnp.arange(n_blk)
    overlap = ((cmp_start[:, None] < (blk[None, :] + 1) * SEL_BLOCK)
               & (cmp_start[:, None] + CMP_LEN > blk[None, :] * SEL_BLOCK))
    imp = jnp.einsum('bghtn,nj->bgtj', p_cmp, overlap.astype(jnp.float32))
    cur = q_pos // SEL_BLOCK
    forced = (blk[None, :] == 0) | (blk[None, :] == cur[:, None]) | (blk[None, :] == cur[:, None] - 1)
    visible = blk[None, :] * SEL_BLOCK <= q_pos[:, None]
    score = jnp.where(forced, 1e9, jnp.where(visible, imp, -1.0))
    top_val, top_idx = lax.top_k(score, n_top)
    top_ok = top_val >= 0

    nb, qb = _query_blocks(T)
    bi = jnp.arange(B)[:, None, None]
    gi = jnp.arange(G)[None, :, None]
    offs = jnp.arange(SEL_BLOCK)
    n_keys = n_top * SEL_BLOCK

    def sel_block(args):
        q_b, idx_b, ok_b, pos_b = args
        tok = (idx_b[..., None] * SEL_BLOCK + offs).reshape(B, G, qb, n_keys)
        ok = jnp.repeat(ok_b, SEL_BLOCK, axis=-1) & (tok <= pos_b[:, None])
        tok_c = jnp.minimum(tok, L - 1).reshape(B, G, qb * n_keys)
        k_g = ks[bi, tok_c, gi].reshape(B, G, qb, n_keys, D)
        v_g = vs[bi, tok_c, gi].reshape(B, G, qb, n_keys, D)
        dist = (pos_b[:, None] - tok).astype(jnp.float32)
        s = (jnp.einsum('bqghd,bgqkd->bghqk', q_b, k_g).astype(jnp.float32) * scale
             - slopes[None, :, :, None, None] * dist[:, :, None])
        pr = _masked_softmax(s, ok[:, :, None])
        return jnp.einsum('bghqk,bgqkd->bqghd', pr.astype(v_g.dtype), v_g)

    o_slc = _from_blocks(lax.map(sel_block, (_to_blocks(q, 1, nb), _to_blocks(top_idx, 2, nb),
                                             _to_blocks(top_ok, 2, nb), q_pos.reshape(nb, qb))))

    pad = win_all.shape[1] - T
    kw = win_all[..., :KVG].reshape(B, pad + T, G, D)
    vw = win_all[..., KVG:].reshape(B, pad + T, G, D)

    def win_block(args):
        q_b, pos_b, b = args
        k_b = lax.dynamic_slice_in_dim(kw, b * qb, pad + qb, axis=1)
        v_b = lax.dynamic_slice_in_dim(vw, b * qb, pad + qb, axis=1)
        kp = lax.dynamic_slice_in_dim(win_pos, b * qb, pad + qb)
        dist = pos_b[:, None] - kp[None, :]
        ok = (kp[None, :] >= 0) & (dist >= 0) & (dist < WINDOW)
        s = (jnp.einsum('bqghd,bkgd->bghqk', q_b, k_b).astype(jnp.float32) * scale
             - slopes[:, :, None, None] * dist.astype(jnp.float32))
        pr = _masked_softmax(s, ok)
        return jnp.einsum('bghqk,bkgd->bqghd', pr.astype(v_b.dtype), v_b)

    o_win = _from_blocks(lax.map(win_block, (_to_blocks(q, 1, nb), q_pos.reshape(nb, qb), jnp.arange(nb))))

    g = gates.reshape(B, T, 3, G, NSA_HPG, 1)
    o = g[:, :, 0] * o_cmp + g[:, :, 1] * o_slc + g[:, :, 2] * o_win
    return o.reshape(B, T, NSA_Q)


def _stick_breaking(q, k, v, q_pos):
    B, T, H, D = q.shape
    L = k.shape[1]
    k_pos = jnp.arange(L)
    nb, qb = _query_blocks(T)
    scale = D ** -0.5

    def sb_block(args):
        q_b, pos_b = args
        z = jnp.einsum('bqhd,bkhd->bhqk', q_b, k).astype(jnp.float32) * scale
        before = k_pos[None, :] < pos_b[:, None]
        neg_log_keep = jnp.where(before, jax.nn.softplus(z), 0.0)
        between = lax.cumsum(neg_log_keep, axis=3, reverse=True) - neg_log_keep
        a = jnp.where(before, jnp.exp(jax.nn.log_sigmoid(z) - between), 0.0)
        return jnp.einsum('bhqk,bkhd->bqhd', a.astype(v.dtype), v)

    return _from_blocks(lax.map(sb_block, (_to_blocks(q, 1, nb), q_pos.reshape(nb, qb))))


def _memory_attention(q, mem_kv):
    B, T, H, D = q.shape
    M = mem_kv.shape[1]
    km = mem_kv[..., :MEM_W].reshape(B, M, H, D)
    vm = mem_kv[..., MEM_W:].reshape(B, M, H, D)
    s = jnp.einsum('bthd,bmhd->bhtm', q, km).astype(jnp.float32) * (D ** -0.5)
    pr = jax.nn.softmax(s, axis=-1)
    return jnp.einsum('bhtm,bmhd->bthd', pr.astype(vm.dtype), vm).reshape(B, T, MEM_W)


def _mixer_block(x, mem_kv, past_kv, win_buf, pos0, p):
    B, T, _ = x.shape
    h = x @ p['w_in'] + p['b_in']
    q_nsa = h[..., :OFF_KV].reshape(B, T, NSA_GROUPS, NSA_HPG, HEAD_DIM)
    kv_new = h[..., OFF_KV:OFF_WIN]
    win_new = h[..., OFF_WIN:OFF_QSB]
    q_sb = h[..., OFF_QSB:OFF_QMEM].reshape(B, T, SB_HEADS, HEAD_DIM)
    q_mem = h[..., OFF_QMEM:OFF_GNSA].reshape(B, T, MEM_HEADS, HEAD_DIM)
    g_nsa = jax.nn.sigmoid(h[..., OFF_GNSA:OFF_GMERGE]).reshape(B, T, 3, NSA_HEADS)
    g_merge = jax.nn.sigmoid(h[..., OFF_GMERGE:]).reshape(B, T, 3, D_MODEL)

    kv_all = kv_new if past_kv is None else jnp.concatenate([past_kv, kv_new], axis=1)
    win_all = jnp.concatenate([win_buf, win_new], axis=1)
    pad = win_buf.shape[1]
    q_pos = pos0 + jnp.arange(T)
    win_pos = pos0 - pad + jnp.arange(pad + T)

    o_nsa = _nsa(q_nsa, kv_all, win_all, g_nsa, q_pos, win_pos, p)
    L = kv_all.shape[1]
    k_sb = kv_all[..., 4 * KVG:4 * KVG + SB_W].reshape(B, L, SB_HEADS, HEAD_DIM)
    v_sb = kv_all[..., 4 * KVG + SB_W:].reshape(B, L, SB_HEADS, HEAD_DIM)
    o_sb = _stick_breaking(q_sb, k_sb, v_sb, q_pos).reshape(B, T, SB_W)
    o_mem = _memory_attention(q_mem, mem_kv)

    merged = (g_merge[:, :, 0] * (o_nsa @ p['w_br_nsa'])
              + g_merge[:, :, 1] * (o_sb @ p['w_br_sb'])
              + g_merge[:, :, 2] * (o_mem @ p['w_br_mem']))
    return merged @ p['w_o'], kv_new, win_all


def _conv_ffn(x, conv_buf, p):
    T = x.shape[1]
    u = x @ p['w_up'] + p['b_up']
    u_all = jnp.concatenate([conv_buf, u], axis=1)
    c = p['b_conv'] + p['w_conv'][0] * u_all[:, 0:T]
    for i in range(1, CONV_W):
        c = c + p['w_conv'][i] * u_all[:, i:i + T]
    a, g = jnp.split(c, 2, axis=-1)
    out = (a * jax.nn.gelu(g)) @ p['w_down'] + p['b_down']
    return out, u_all[:, -(CONV_W - 1):]


def _layer(x, mem_kv, past_kv, win_buf, conv_buf, pos0, p):
    mix, kv_new, win_all = _mixer_block(x, mem_kv, past_kv, win_buf, pos0, p)
    x1 = _layer_norm(ALPHA * x + mix, p['ln1_g'], p['ln1_b'])
    ff, conv_new = _conv_ffn(x1, conv_buf, p)
    y = _layer_norm(ALPHA * x1 + ff, p['ln2_g'], p['ln2_b'])
    return y, kv_new, win_all, conv_new


def setup_inputs(seed: int = 0) -> dict:
    key = jax.random.key(seed)
    keys = iter(jax.random.split(key, 48))

    def nrm(shape, scale=1.0):
        return scale * jax.random.normal(next(keys), shape, jnp.float32)

    n_pages = PAST_LEN // PAGE_SIZE
    n_phys = (DEC_BATCH * n_pages * 5) // 4
    w_buf = min(WINDOW, PAST_LEN)
    Dp = DEPTH
    D = HEAD_DIM
    return {
        'x_prompt': nrm((BATCH, SEQ, D_MODEL)),
        'x_sample': nrm((DEC_BATCH, DEC_SEQ, D_MODEL)),
        'mem_prompt': nrm((BATCH, MEM_TOKENS, D_MODEL)),
        'cache_kv_pages': nrm((Dp, n_phys, PAGE_SIZE, KV_CH)),
        'page_table': jax.random.permutation(next(keys), n_phys)[:DEC_BATCH * n_pages]
                      .reshape(DEC_BATCH, n_pages).astype(jnp.int32),
        'cache_win_kv': nrm((Dp, DEC_BATCH, w_buf, WIN_CH)),
        'cache_mem_kv': nrm((Dp, DEC_BATCH, MEM_TOKENS, MEM_KV_CH)),
        'state_ffn_conv': nrm((Dp, DEC_BATCH, CONV_W - 1, 2 * D_FF)),
        'w_in': nrm((Dp, D_MODEL, PROJ_W), D_MODEL ** -0.5),
        'b_in': nrm((Dp, PROJ_W), 0.01),
        'w_cmp_k1': nrm((Dp, CMP_LEN, D, D), (CMP_LEN * D) ** -0.5),
        'pe_cmp_k': nrm((Dp, CMP_LEN, D), 0.5),
        'w_cmp_k2': nrm((Dp, D, D), D ** -0.5),
        'w_cmp_v1': nrm((Dp, CMP_LEN, D, D), (CMP_LEN * D) ** -0.5),
        'pe_cmp_v': nrm((Dp, CMP_LEN, D), 0.5),
        'w_cmp_v2': nrm((Dp, D, D), D ** -0.5),
        'w_br_nsa': nrm((Dp, NSA_Q, D_MODEL), BETA * NSA_Q ** -0.5),
        'w_br_sb': nrm((Dp, SB_W, D_MODEL), BETA * SB_W ** -0.5),
        'w_br_mem': nrm((Dp, MEM_W, D_MODEL), BETA * MEM_W ** -0.5),
        'w_o': nrm((Dp, D_MODEL, D_MODEL), BETA * D_MODEL ** -0.5),
        'w_mem_kv': nrm((Dp, D_MODEL, MEM_KV_CH), D_MODEL ** -0.5),
        'b_mem_kv': nrm((Dp, MEM_KV_CH), 0.01),
        'ln1_g': 1.0 + nrm((Dp, D_MODEL), 0.01),
        'ln1_b': nrm((Dp, D_MODEL), 0.01),
        'w_up': nrm((Dp, D_MODEL, 2 * D_FF), D_MODEL ** -0.5),
        'b_up': nrm((Dp, 2 * D_FF), 0.01),
        'w_conv': nrm((Dp, CONV_W, 2 * D_FF), CONV_W ** -0.5),
        'b_conv': nrm((Dp, 2 * D_FF), 0.01),
        'w_down': nrm((Dp, D_FF, D_MODEL), BETA * D_FF ** -0.5),
        'b_down': nrm((Dp, D_MODEL), 0.01),
        'ln2_g': 1.0 + nrm((Dp, D_MODEL), 0.01),
        'ln2_b': nrm((Dp, D_MODEL), 0.01),
    }


def reference(x_prompt, x_sample, mem_prompt, cache_kv_pages, page_table, cache_win_kv, cache_mem_kv,
              state_ffn_conv, w_in, b_in, w_cmp_k1, pe_cmp_k, w_cmp_k2, w_cmp_v1, pe_cmp_v, w_cmp_v2,
              w_br_nsa, w_br_sb, w_br_mem, w_o, w_mem_kv, b_mem_kv, ln1_g, ln1_b, w_up, b_up, w_conv,
              b_conv, w_down, b_down, ln2_g, ln2_b):
    n_batch, n_seq = x_prompt.shape[0], x_prompt.shape[1]
    n_dec, n_pages = page_table.shape
    past_len = n_pages * cache_kv_pages.shape[2]
    w_buf = cache_win_kv.shape[2]
    hp, hs = x_prompt, x_sample
    kv_p, win_p, mem_p, conv_p, kv_s, win_s, conv_s = [], [], [], [], [], [], []
    for l in range(DEPTH):
        p = dict(w_in=w_in[l], b_in=b_in[l], w_cmp_k1=w_cmp_k1[l], pe_cmp_k=pe_cmp_k[l],
                 w_cmp_k2=w_cmp_k2[l], w_cmp_v1=w_cmp_v1[l], pe_cmp_v=pe_cmp_v[l], w_cmp_v2=w_cmp_v2[l],
                 w_br_nsa=w_br_nsa[l], w_br_sb=w_br_sb[l], w_br_mem=w_br_mem[l], w_o=w_o[l],
                 ln1_g=ln1_g[l], ln1_b=ln1_b[l], w_up=w_up[l], b_up=b_up[l], w_conv=w_conv[l],
                 b_conv=b_conv[l], w_down=w_down[l], b_down=b_down[l], ln2_g=ln2_g[l], ln2_b=ln2_b[l])
        mem_kv = mem_prompt @ w_mem_kv[l] + b_mem_kv[l]
        win0 = jnp.zeros((n_batch, WINDOW, WIN_CH), hp.dtype)
        conv0 = jnp.zeros((n_batch, CONV_W - 1, 2 * D_FF), hp.dtype)
        hp, kv_new, win_all, conv_new = _layer(hp, mem_kv, None, win0, conv0, 0, p)
        kv_p.append(kv_new)
        win_p.append(win_all[:, -min(WINDOW, n_seq):])
        mem_p.append(mem_kv)
        conv_p.append(conv_new)
        past = cache_kv_pages[l][page_table].reshape(n_dec, past_len, KV_CH)
        hs, kv_new, win_all, conv_new = _layer(hs, cache_mem_kv[l], past, cache_win_kv[l],
                                               state_ffn_conv[l], past_len, p)
        kv_s.append(kv_new)
        win_s.append(win_all[:, -w_buf:])
        conv_s.append(conv_new)
    return (hp, hs, jnp.stack(kv_p), jnp.stack(win_p), jnp.stack(mem_p), jnp.stack(conv_p),
            jnp.stack(kv_s), jnp.stack(win_s), jnp.stack(conv_s))
```

```python
import functools
import math

import jax
import jax.numpy as jnp
from jax import lax
from jax.experimental import pallas as pl
from jax.experimental.pallas import tpu as pltpu

F32 = jnp.float32
BF16 = jnp.bfloat16
NEG = -1e30

HEAD_DIM = 128
NSA_HEADS = 8
NSA_GROUPS = 2
NSA_HPG = NSA_HEADS // NSA_GROUPS
CMP_LEN = 32
CMP_STRIDE = 16
SEL_BLOCK = 64
N_SEL = 16
WINDOW = 512
SB_HEADS = 4
MEM_HEADS = 4
CONV_W = 3
QB = 128
TPAD = 8
LN_EPS = 1e-5
VMEM_LIMIT = 56 * 1024 * 1024

NSA_Q = NSA_HEADS * HEAD_DIM
KVG = NSA_GROUPS * HEAD_DIM
SB_W = SB_HEADS * HEAD_DIM
MEM_W = MEM_HEADS * HEAD_DIM
KV_CH = 4 * KVG + 2 * SB_W
WIN_CH = 2 * KVG
OFF_KV = NSA_Q
OFF_WIN = OFF_KV + KV_CH
OFF_QSB = OFF_WIN + WIN_CH
OFF_GNSA = OFF_QSB + SB_W + MEM_W
OFF_GMERGE = OFF_GNSA + 3 * NSA_HEADS
SCALE = HEAD_DIM ** -0.5


def _cparams(*sem):
    return pltpu.CompilerParams(dimension_semantics=sem, vmem_limit_bytes=VMEM_LIMIT)


def _nt(a, b):
    return lax.dot_general(a, b, (((1,), (1,)), ((), ())), preferred_element_type=F32)


def _tn(a, b):
    return lax.dot_general(a, b, (((0,), (0,)), ((), ())), preferred_element_type=F32)


def _dot(a, b):
    return jnp.dot(a, b, preferred_element_type=F32)


def _iota(shape, dim):
    return lax.broadcasted_iota(jnp.int32, shape, dim)


def _softmax_rows(s, mask):
    s = jnp.where(mask, s, NEG)
    m = jnp.max(s, axis=-1, keepdims=True)
    e = jnp.where(mask, jnp.exp(s - m), 0.0)
    return e / jnp.maximum(jnp.sum(e, axis=-1, keepdims=True), 1e-30)


def _split3(x):
    hi = x.astype(BF16)
    r1 = x - hi.astype(F32)
    mid = r1.astype(BF16)
    lo = (r1 - mid.astype(F32)).astype(BF16)
    return hi, mid, lo


def _softplus_pair(z):
    lg = jnp.log1p(jnp.exp(-jnp.abs(z)))
    return jnp.maximum(z, 0.0) + lg, jnp.minimum(z, 0.0) - lg


def _topk_mask(score, lane, n_top):
    sel = jnp.zeros(score.shape, F32)
    for _ in range(n_top):
        mx = jnp.max(score, axis=-1, keepdims=True)
        idx = jnp.min(jnp.where(score == mx, lane, 1 << 20), axis=-1, keepdims=True)
        hit = lane == idx
        sel = jnp.where(hit & (mx >= 0.0), 1.0, sel)
        score = jnp.where(hit, -3.0, score)
    return sel


def _block_scores(imp, pos, lane, n_blk):
    cur = pos // SEL_BLOCK
    forced = (lane == 0) | (lane == cur) | (lane == cur - 1)
    visible = lane * SEL_BLOCK <= pos
    score = jnp.where(forced, 1e9, jnp.where(visible, imp, -1.0))
    return jnp.where(lane < n_blk, score, -2.0)


def _overlap(nsub, lanes, n_cmp, n_blk):
    n = _iota((nsub, lanes), 0)
    j = _iota((nsub, lanes), 1)
    st = n * CMP_STRIDE
    ov = (st < (j + 1) * SEL_BLOCK) & (st + CMP_LEN > j * SEL_BLOCK) & (n < n_cmp) & (j < n_blk)
    return jnp.where(ov, 1.0, 0.0).astype(BF16)


def _mm_body(x_ref, w_ref, b_ref, *o_refs, act):
    r = _dot(x_ref[...], w_ref[...]) + b_ref[...]
    if act == "sigmoid":
        r = jax.nn.sigmoid(r)
    for o in o_refs:
        o[...] = r.astype(o.dtype)


def _matmul(x, w, b, out_dtypes, act=None, tm=1024, tn=512):
    M, K = x.shape
    N = w.shape[1]
    tm, tn = min(tm, M), min(tn, N)
    assert M % tm == 0 and N % tn == 0
    return pl.pallas_call(
        functools.partial(_mm_body, act=act),
        grid=(M // tm, N // tn),
        in_specs=[pl.BlockSpec((tm, K), lambda i, j: (i, 0)),
                  pl.BlockSpec((K, tn), lambda i, j: (0, j)),
                  pl.BlockSpec((1, tn), lambda i, j: (0, j))],
        out_specs=[pl.BlockSpec((tm, tn), lambda i, j: (i, j)) for _ in out_dtypes],
        out_shape=[jax.ShapeDtypeStruct((M, N), d) for d in out_dtypes],
        compiler_params=_cparams("parallel", "parallel"),
    )(x, w, b.reshape(1, N).astype(F32))


def _mm_ln_body(x_ref, w_ref, b_ref, res_ref, g_ref, be_ref, *rest, nk, alpha, n_out):
    o_refs, acc_ref = rest[:n_out], rest[n_out]
    k = pl.program_id(1)

    @pl.when(k == 0)
    def _():
        acc_ref[...] = jnp.zeros_like(acc_ref)

    acc_ref[...] += _dot(x_ref[...], w_ref[...])

    @pl.when(k == nk - 1)
    def _():
        v = alpha * res_ref[...] + (acc_ref[...] + b_ref[...])
        mu = jnp.mean(v, axis=-1, keepdims=True)
        d = v - mu
        var = jnp.mean(d * d, axis=-1, keepdims=True)
        y = d * lax.rsqrt(var + LN_EPS) * g_ref[...] + be_ref[...]
        for o in o_refs:
            o[...] = y.astype(o.dtype)


def _matmul_ln(x, w, b, res, g, be, alpha, out_dtypes, tm=512, tk=512):
    M, K = x.shape
    N = w.shape[1]
    tm, tk = min(tm, M), min(tk, K)
    assert M % tm == 0 and K % tk == 0
    nk = K // tk
    row = lambda a: a.reshape(1, N).astype(F32)
    return pl.pallas_call(
        functools.partial(_mm_ln_body, nk=nk, alpha=alpha, n_out=len(out_dtypes)),
        grid=(M // tm, nk),
        in_specs=[pl.BlockSpec((tm, tk), lambda i, k: (i, k)),
                  pl.BlockSpec((tk, N), lambda i, k: (k, 0)),
                  pl.BlockSpec((1, N), lambda i, k: (0, 0)),
                  pl.BlockSpec((tm, N), lambda i, k: (i, 0)),
                  pl.BlockSpec((1, N), lambda i, k: (0, 0)),
                  pl.BlockSpec((1, N), lambda i, k: (0, 0))],
        out_specs=[pl.BlockSpec((tm, N), lambda i, k: (i, 0)) for _ in out_dtypes],
        out_shape=[jax.ShapeDtypeStruct((M, N), d) for d in out_dtypes],
        scratch_shapes=[pltpu.VMEM((tm, N), F32)],
        compiler_params=_cparams("parallel", "arbitrary"),
    )(x, w, row(b), res, row(g), row(be))


def _merge_body(on_ref, os_ref, om_ref, g0_ref, g1_ref, g2_ref, wn_ref, ws_ref, wm_ref, o_ref):
    r = (g0_ref[...] * _dot(on_ref[...], wn_ref[...])
         + g1_ref[...] * _dot(os_ref[...], ws_ref[...])
         + g2_ref[...] * _dot(om_ref[...], wm_ref[...]))
    o_ref[...] = r.astype(o_ref.dtype)


def _merge(o_nsa, o_sb, o_mem, gm, w_nsa, w_sb, w_mem, tm=1024, tn=512):
    M = o_nsa.shape[0]
    D = w_nsa.shape[1]
    tm, tn = min(tm, M), min(tn, D)
    nj = D // tn
    xs = lambda a: pl.BlockSpec((tm, a.shape[1]), lambda i, j: (i, 0))
    ws = lambda a: pl.BlockSpec((a.shape[0], tn), lambda i, j: (0, j))
    gs = lambda c: pl.BlockSpec((tm, tn), lambda i, j: (i, c * nj + j))
    return pl.pallas_call(
        _merge_body,
        grid=(M // tm, nj),
        in_specs=[xs(o_nsa), xs(o_sb), xs(o_mem), gs(0), gs(1), gs(2), ws(w_nsa), ws(w_sb), ws(w_mem)],
        out_specs=pl.BlockSpec((tm, tn), lambda i, j: (i, j)),
        out_shape=jax.ShapeDtypeStruct((M, D), BF16),
        compiler_params=_cparams("parallel", "parallel"),
    )(o_nsa, o_sb, o_mem, gm, gm, gm, w_nsa, w_sb, w_mem)


def _conv_taps(u, p1, p2, wc_ref, bc_ref):
    return bc_ref[...] + wc_ref[0:1, :] * p2 + wc_ref[1:2, :] * p1 + wc_ref[2:3, :] * u


def _ffn_up_seq_body(x_ref, wa_ref, wg_ref, ba_ref, bg_ref, wca_ref, wcg_ref, bca_ref, bcg_ref, ha_ref, hg_ref,
                     act_ref, sa_ref, sg_ref, ca_ref, cg_ref, *, nt, tm):
    t = pl.program_id(2)

    @pl.when(t == 0)
    def _():
        ca_ref[0:2, :] = ha_ref[0]
        cg_ref[0:2, :] = hg_ref[0]

    x = x_ref[0]
    r = _iota((tm, 1), 0)

    def half(w_ref, b_ref, wc_ref, bc_ref, c_ref, s_ref):
        u = _dot(x, w_ref[...]) + b_ref[...]
        h0, h1 = c_ref[0:1, :], c_ref[1:2, :]
        p1 = jnp.where(r == 0, h1, pltpu.roll(u, 1, 0))
        p2 = jnp.where(r == 0, h0, jnp.where(r == 1, h1, pltpu.roll(u, 2, 0)))
        c = _conv_taps(u, p1, p2, wc_ref, bc_ref)
        c_ref[0:2, :] = u[tm - 2:tm, :]

        @pl.when(t == nt - 1)
        def _():
            s_ref[0] = u[tm - 2:tm, :]

        return c

    a = half(wa_ref, ba_ref, wca_ref, bca_ref, ca_ref, sa_ref)
    g = half(wg_ref, bg_ref, wcg_ref, bcg_ref, cg_ref, sg_ref)
    act_ref[0] = (a * jax.nn.gelu(g, approximate=True)).astype(act_ref.dtype)


def _ffn_up_seq(x, w_a, w_g, b_a, b_g, wc_a, wc_g, bc_a, bc_g, h_a, h_g, tm=1024, tn=512):
    B, T, D = x.shape
    Fh = w_a.shape[1]
    tm, tn = min(tm, T), min(tn, Fh)
    assert T % tm == 0 and Fh % tn == 0 and tm >= 8
    nt = T // tm
    wsp = pl.BlockSpec((D, tn), lambda j, b, t: (0, j))
    rsp = pl.BlockSpec((1, tn), lambda j, b, t: (0, j))
    csp = pl.BlockSpec((CONV_W, tn), lambda j, b, t: (0, j))
    hsp = pl.BlockSpec((1, 2, tn), lambda j, b, t: (b, 0, j))
    row = lambda a: a.reshape(1, Fh)
    return pl.pallas_call(
        functools.partial(_ffn_up_seq_body, nt=nt, tm=tm),
        grid=(Fh // tn, B, nt),
        in_specs=[pl.BlockSpec((1, tm, D), lambda j, b, t: (b, t, 0)), wsp, wsp, rsp, rsp, csp, csp, rsp, rsp, hsp, hsp],
        out_specs=[pl.BlockSpec((1, tm, tn), lambda j, b, t: (b, t, j)), hsp, hsp],
        out_shape=[jax.ShapeDtypeStruct((B, T, Fh), BF16),
                   jax.ShapeDtypeStruct((B, 2, Fh), F32), jax.ShapeDtypeStruct((B, 2, Fh), F32)],
        scratch_shapes=[pltpu.VMEM((8, tn), F32), pltpu.VMEM((8, tn), F32)],
        compiler_params=_cparams("parallel", "parallel", "arbitrary"),
    )(x, w_a, w_g, row(b_a), row(b_g), wc_a, wc_g, row(bc_a), row(bc_g), h_a, h_g)


def _ffn_up_rows_body(x_ref, wa_ref, wg_ref, ba_ref, bg_ref, wca_ref, wcg_ref, bca_ref, bcg_ref,
                      h1a_ref, h2a_ref, h1g_ref, h2g_ref, act_ref, ua_ref, ug_ref, *, period):
    x = x_ref[...]
    tin = _iota((x.shape[0], 1), 0) % period

    def half(w_ref, b_ref, wc_ref, bc_ref, h1_ref, h2_ref, u_ref):
        u = _dot(x, w_ref[...]) + b_ref[...]
        u_ref[...] = u
        p1 = jnp.where(tin >= 1, pltpu.roll(u, 1, 0), h1_ref[...])
        p2 = jnp.where(tin >= 2, pltpu.roll(u, 2, 0), h2_ref[...])
        return _conv_taps(u, p1, p2, wc_ref, bc_ref)

    a = half(wa_ref, ba_ref, wca_ref, bca_ref, h1a_ref, h2a_ref, ua_ref)
    g = half(wg_ref, bg_ref, wcg_ref, bcg_ref, h1g_ref, h2g_ref, ug_ref)
    act_ref[...] = (a * jax.nn.gelu(g, approximate=True)).astype(act_ref.dtype)


def _ffn_up_rows(x, w_a, w_g, b_a, b_g, wc_a, wc_g, bc_a, bc_g, h1a, h2a, h1g, h2g, period, tn=512):
    M, D = x.shape
    Fh = w_a.shape[1]
    tn = min(tn, Fh)
    wsp = pl.BlockSpec((D, tn), lambda j: (0, j))
    rsp = pl.BlockSpec((1, tn), lambda j: (0, j))
    csp = pl.BlockSpec((CONV_W, tn), lambda j: (0, j))
    msp = pl.BlockSpec((M, tn), lambda j: (0, j))
    row = lambda a: a.reshape(1, Fh)
    return pl.pallas_call(
        functools.partial(_ffn_up_rows_body, period=period),
        grid=(Fh // tn,),
        in_specs=[pl.BlockSpec((M, D), lambda j: (0, 0)), wsp, wsp, rsp, rsp, csp, csp, rsp, rsp, msp, msp, msp, msp],
        out_specs=[msp, msp, msp],
        out_shape=[jax.ShapeDtypeStruct((M, Fh), BF16), jax.ShapeDtypeStruct((M, Fh), F32),
                   jax.ShapeDtypeStruct((M, Fh), F32)],
        compiler_params=_cparams("parallel"),
    )(x, w_a, w_g, row(b_a), row(b_g), wc_a, wc_g, row(bc_a), row(bc_g), h1a, h2a, h1g, h2g)


def _compress_body(pt_ref, page_ref, w1k_ref, w1v_ref, pek_ref, pev_ref, w2k_ref, w2v_ref, ok_ref, ov_ref,
                   stage_ref, xs_ref, *, n_pages, nsub):
    p = pl.program_id(1)
    sub = QB // CMP_STRIDE
    ncb = stage_ref.shape[0]
    for cb in range(ncb):
        stage_ref[cb] = page_ref[0, :, cb * 128:(cb + 1) * 128]
    for s in range(CMP_STRIDE):
        for cb in range(ncb):
            rows = stage_ref[pl.ds(cb, 1), pl.ds(s, sub, stride=CMP_STRIDE), :]
            xs_ref[pl.ds(s, 1), pl.ds(pl.multiple_of(p * sub, sub), sub), cb * 128:(cb + 1) * 128] = rows

    @pl.when(p == n_pages - 1)
    def _():
        for kv, (w1_ref, pe_ref, w2_ref, o_ref) in enumerate(((w1k_ref, pek_ref, w2k_ref, ok_ref),
                                                              (w1v_ref, pev_ref, w2v_ref, ov_ref))):
            acc = [jnp.zeros((NSA_GROUPS * nsub, HEAD_DIM), F32) for _ in range(CMP_LEN // CMP_STRIDE)]
            for s in range(CMP_STRIDE):
                x = xs_ref[s, :, kv * KVG:(kv + 1) * KVG]
                xg = jnp.concatenate([x[:, g * HEAD_DIM:(g + 1) * HEAD_DIM] for g in range(NSA_GROUPS)], axis=0)
                for r in range(CMP_LEN // CMP_STRIDE):
                    i = r * CMP_STRIDE + s
                    acc[r] = acc[r] + _dot((xg + pe_ref[i:i + 1, :]).astype(BF16), w1_ref[i])
            hid = acc[0] + pltpu.roll(acc[1], NSA_GROUPS * nsub - 1, 0)
            out = _dot(jax.nn.gelu(hid, approximate=True).astype(BF16), w2_ref[...])
            for g in range(NSA_GROUPS):
                o_ref[0, g] = out[g * nsub:(g + 1) * nsub, :].astype(o_ref.dtype)


def _compress(pages, table, w1k, w1v, pek, pev, w2k, w2v):
    B, P = table.shape
    nsub = P * (QB // CMP_STRIDE)
    assert CMP_LEN == 2 * CMP_STRIDE
    full = lambda a: pl.BlockSpec(a.shape, lambda b, p, pt: (0,) * a.ndim)
    osp = pl.BlockSpec((1, NSA_GROUPS, nsub, HEAD_DIM), lambda b, p, pt: (b, 0, 0, 0))
    osh = jax.ShapeDtypeStruct((B, NSA_GROUPS, nsub, HEAD_DIM), BF16)
    return pl.pallas_call(
        functools.partial(_compress_body, n_pages=P, nsub=nsub),
        grid_spec=pltpu.PrefetchScalarGridSpec(
            num_scalar_prefetch=1, grid=(B, P),
            in_specs=[pl.BlockSpec((1, QB, 2 * KVG), lambda b, p, pt: (pt[b, p], 0, 0)),
                      full(w1k), full(w1v), full(pek), full(pev), full(w2k), full(w2v)],
            out_specs=[osp, osp],
            scratch_shapes=[pltpu.VMEM((2 * KVG // 128, QB, 128), F32),
                            pltpu.VMEM((CMP_STRIDE, nsub, 2 * KVG), F32)]),
        out_shape=[osh, osh],
        compiler_params=_cparams("parallel", "arbitrary"),
    )(table, pages, w1k, w1v, pek, pev, w2k, w2v)


def _nsa_seq_body(q_ref, kc_ref, vc_ref, ks_ref, vs_ref, kw_ref, vw_ref, gate_ref, slope_ref, o_ref,
                  m_ref, l_ref, acc_ref, sel_ref, *, n_cmp, n_blk, n_top):
    g = pl.program_id(1)
    i = pl.program_id(2)
    t0 = i * QB
    R = NSA_HPG * QB
    q = q_ref[0]
    qs = jnp.concatenate([q[:, h * HEAD_DIM:(h + 1) * HEAD_DIM] for h in range(NSA_HPG)], axis=0)
    pos = t0 + _iota((R, 1), 0) % QB
    slope = slope_ref[0]
    lane = _iota((1, QB), 1)

    kc = kc_ref[0, 0]
    nsub = kc.shape[0]
    ncol = _iota((1, nsub), 1)
    dist_c = pos - (ncol * CMP_STRIDE + (CMP_LEN - 1))
    s = _nt(qs, kc) * SCALE - slope * dist_c.astype(F32)
    p_c = _softmax_rows(s, (dist_c >= 0) & (ncol < n_cmp)).astype(BF16)
    o_cmp = _dot(p_c, vc_ref[0, 0])
    imp4 = _dot(p_c, _overlap(nsub, QB, n_cmp, n_blk))
    imp = imp4[0:QB]
    for h in range(1, NSA_HPG):
        imp = imp + imp4[h * QB:(h + 1) * QB]
    score = _block_scores(imp, t0 + _iota((QB, 1), 0), lane, n_blk)
    sel_ref[...] = _topk_mask(score, lane, n_top).astype(BF16)

    def attend(lo, hi, k_ref, v_ref, mask_fn):
        m_ref[...] = jnp.full(m_ref.shape, NEG, F32)
        l_ref[...] = jnp.zeros_like(l_ref)
        acc_ref[...] = jnp.zeros_like(acc_ref)

        def body(c, carry):
            k0 = pl.multiple_of(c * QB, QB)
            dist = pos - (k0 + lane)
            sc = _nt(qs, k_ref[0, pl.ds(k0, QB), :]) * SCALE - slope * dist.astype(F32)
            mask = mask_fn(c, dist)
            sc = jnp.where(mask, sc, NEG)
            m_old = m_ref[...]
            m_new = jnp.maximum(m_old, jnp.max(sc, axis=-1, keepdims=True))
            a = jnp.exp(m_old - m_new)
            p = jnp.where(mask, jnp.exp(sc - m_new), 0.0)
            l_ref[...] = a * l_ref[...] + jnp.sum(p, axis=-1, keepdims=True)
            acc_ref[...] = a * acc_ref[...] + _dot(p.astype(BF16), v_ref[0, pl.ds(k0, QB), :])
            m_ref[...] = m_new
            return carry

        lax.fori_loop(lo, hi, body, 0)
        return acc_ref[...] / jnp.maximum(l_ref[...], 1e-30)

    def slc_mask(c, dist):
        j = _iota((QB, QB), 0)
        k = _iota((QB, QB), 1)
        e = jnp.where(j == 2 * c + k // SEL_BLOCK, 1.0, 0.0).astype(BF16)
        mk = _dot(sel_ref[...], e)
        mk = jnp.concatenate([mk] * NSA_HPG, axis=0)
        return (mk > 0.5) & (dist >= 0)

    o_slc = attend(0, i + 1, ks_ref, vs_ref, slc_mask)
    o_win = attend(jnp.maximum(i - WINDOW // QB, 0), i + 1, kw_ref, vw_ref,
                   lambda c, dist: (dist >= 0) & (dist < WINDOW))

    gate = gate_ref[0]
    outs = []
    for h in range(NSA_HPG):
        rows = slice(h * QB, (h + 1) * QB)
        o_h = jnp.zeros((QB, HEAD_DIM), F32)
        for br, o_br in enumerate((o_cmp, o_slc, o_win)):
            col = br * NSA_HEADS + g * NSA_HPG + h
            gcol = jnp.sum(jnp.where(lane == col, gate, 0.0), axis=-1, keepdims=True)
            o_h = o_h + gcol * o_br[rows]
        outs.append(o_h)
    o_ref[0] = jnp.concatenate(outs, axis=-1).astype(o_ref.dtype)


def _nsa_seq(q, kcmp, vcmp, kvb, winb, gates, slopes):
    B, T, _ = q.shape
    nq = T // QB
    nsub = kcmp.shape[2]
    n_blk = -(-T // SEL_BLOCK)
    assert T % QB == 0 and n_blk <= QB and WINDOW % QB == 0
    R = NSA_HPG * QB
    csp = pl.BlockSpec((1, 1, nsub, HEAD_DIM), lambda b, g, i: (b, g, 0, 0))
    col = lambda c0: pl.BlockSpec((1, T, HEAD_DIM), lambda b, g, i: (b, 0, c0 + g))
    return pl.pallas_call(
        functools.partial(_nsa_seq_body, n_cmp=nsub - 1, n_blk=n_blk, n_top=min(N_SEL, n_blk)),
        grid=(B, NSA_GROUPS, nq),
        in_specs=[pl.BlockSpec((1, QB, NSA_HPG * HEAD_DIM), lambda b, g, i: (b, i, g)), csp, csp,
                  col(2 * NSA_GROUPS), col(3 * NSA_GROUPS), col(0), col(NSA_GROUPS),
                  pl.BlockSpec((1, QB, 128), lambda b, g, i: (b, i, 0)),
                  pl.BlockSpec((1, R, 1), lambda b, g, i: (g, 0, 0))],
        out_specs=pl.BlockSpec((1, QB, NSA_HPG * HEAD_DIM), lambda b, g, i: (b, i, g)),
        out_shape=jax.ShapeDtypeStruct((B, T, NSA_Q), BF16),
        scratch_shapes=[pltpu.VMEM((R, 1), F32), pltpu.VMEM((R, 1), F32), pltpu.VMEM((R, HEAD_DIM), F32),
                        pltpu.VMEM((QB, QB), BF16)],
        compiler_params=_cparams("parallel", "parallel", "arbitrary"),
    )(q, kcmp, vcmp, kvb, kvb, winb, winb, gates, slopes)


def _sbmem_seq_body(qs_ref, qm_ref, k_ref, v_ref, km_ref, vm_ref, osb_ref, om_ref, r_ref, acc_ref):
    i = pl.program_id(2)
    t0 = i * QB
    q = qs_ref[0]
    pos = t0 + _iota((QB, 1), 0)
    lane = _iota((1, QB), 1)
    upper = jnp.where(_iota((QB, QB), 0) > _iota((QB, QB), 1), 1.0, 0.0).astype(BF16)
    r_ref[...] = jnp.zeros_like(r_ref)
    acc_ref[...] = jnp.zeros_like(acc_ref)

    def body(step, carry):
        k0 = pl.multiple_of((i - step) * QB, QB)
        before = (pos - (k0 + lane)) > 0
        z = _nt(q, k_ref[0, pl.ds(k0, QB), :]) * SCALE
        sp, ls = _softplus_pair(z)
        nlk = jnp.where(before, sp, 0.0)
        hi, mid, lo = _split3(nlk)
        later = _dot(hi, upper) + _dot(mid, upper) + _dot(lo, upper)
        a = jnp.where(before, jnp.exp(ls - (r_ref[...] + later)), 0.0)
        acc_ref[...] += _dot(a.astype(BF16), v_ref[0, pl.ds(k0, QB), :])
        r_ref[...] += jnp.sum(nlk, axis=-1, keepdims=True)
        return carry

    lax.fori_loop(0, i + 1, body, 0)
    osb_ref[0] = acc_ref[...].astype(osb_ref.dtype)

    s = _nt(qm_ref[0], km_ref[0]) * SCALE
    p = _softmax_rows(s, jnp.full(s.shape, True))
    om_ref[0] = _dot(p.astype(BF16), vm_ref[0]).astype(om_ref.dtype)


def _sbmem_seq(qsm, kvb, memb):
    B, T, _ = qsm.shape
    Mt = memb.shape[1]
    nq = T // QB
    qsp = lambda c0: pl.BlockSpec((1, QB, HEAD_DIM), lambda b, h, i: (b, i, c0 + h))
    ksp = lambda c0: pl.BlockSpec((1, T, HEAD_DIM), lambda b, h, i: (b, 0, c0 + h))
    msp = lambda c0: pl.BlockSpec((1, Mt, HEAD_DIM), lambda b, h, i: (b, 0, c0 + h))
    osp = pl.BlockSpec((1, QB, HEAD_DIM), lambda b, h, i: (b, i, h))
    sb0 = 4 * NSA_GROUPS
    return pl.pallas_call(
        _sbmem_seq_body,
        grid=(B, SB_HEADS, nq),
        in_specs=[qsp(0), qsp(SB_HEADS), ksp(sb0), ksp(sb0 + SB_HEADS), msp(0), msp(MEM_HEADS)],
        out_specs=[osp, osp],
        out_shape=[jax.ShapeDtypeStruct((B, T, SB_W), BF16), jax.ShapeDtypeStruct((B, T, MEM_W), BF16)],
        scratch_shapes=[pltpu.VMEM((QB, 1), F32), pltpu.VMEM((QB, HEAD_DIM), F32)],
        compiler_params=_cparams("parallel", "parallel", "arbitrary"),
    )(qsm, qsm, kvb, kvb, memb, memb)


def _dec_local_body(q_ref, kc_ref, vc_ref, win_ref, qm_ref, mem_ref, slope_ref, ocmp_ref, owin_ref, omem_ref, sel_ref,
                    *, n_cmp, n_blk, n_top, pos0, n_hist, n_new):
    R = NSA_HPG * TPAD
    row = _iota((R, 1), 0)
    pos = pos0 + row % TPAD
    nsub = kc_ref.shape[2]
    lanes = sel_ref.shape[3]
    lane = _iota((1, lanes), 1)
    ncol = _iota((1, nsub), 1)
    nwin = win_ref.shape[1]
    kidx = _iota((1, nwin), 1)
    kp = pos0 - n_hist + kidx
    ov = _overlap(nsub, lanes, n_cmp, n_blk)
    for g in range(NSA_GROUPS):
        q = q_ref[0, g]
        slope = slope_ref[g]
        dist_c = pos - (ncol * CMP_STRIDE + (CMP_LEN - 1))
        s = _nt(q, kc_ref[0, g]) * SCALE - slope * dist_c.astype(F32)
        p_c = _softmax_rows(s, (dist_c >= 0) & (ncol < n_cmp)).astype(BF16)
        ocmp_ref[0, g] = _dot(p_c, vc_ref[0, g])
        imp4 = _dot(p_c, ov)
        imp = imp4[0:TPAD]
        for h in range(1, NSA_HPG):
            imp = imp + imp4[h * TPAD:(h + 1) * TPAD]
        score = _block_scores(imp, pos0 + _iota((TPAD, 1), 0), lane, n_blk)
        sel_ref[0, g] = _topk_mask(score, lane, n_top)

        kw = win_ref[0, :, g * HEAD_DIM:(g + 1) * HEAD_DIM].astype(BF16)
        vw = win_ref[0, :, KVG + g * HEAD_DIM:KVG + (g + 1) * HEAD_DIM].astype(BF16)
        dist_w = pos - kp
        s = _nt(q, kw) * SCALE - slope * dist_w.astype(F32)
        ok = (kidx < n_hist + n_new) & (kp >= 0) & (dist_w >= 0) & (dist_w < WINDOW)
        owin_ref[0, g] = _dot(_softmax_rows(s, ok).astype(BF16), vw)

    km = mem_ref[0, :, :MEM_W].astype(BF16)
    vm = mem_ref[0, :, MEM_W:].astype(BF16)
    s = _nt(qm_ref[0], km) * SCALE
    o_all = _dot(_softmax_rows(s, jnp.full(s.shape, True)).astype(BF16), vm)
    hrow = _iota((MEM_HEADS * TPAD, 1), 0) // TPAD
    o = jnp.zeros((MEM_HEADS * TPAD, HEAD_DIM), F32)
    for h in range(MEM_HEADS):
        o = o + jnp.where(hrow == h, o_all[:, h * HEAD_DIM:(h + 1) * HEAD_DIM], 0.0)
    omem_ref[0] = o


def _dec_local(q, kcmp, vcmp, win_pad, qm_bd, mem, slopes, pos0, n_hist, n_new, n_blk):
    B = q.shape[0]
    R = NSA_HPG * TPAD
    nsub = kcmp.shape[2]
    lanes = 128 * (-(-n_blk // 128))
    b4 = lambda a: pl.BlockSpec((1,) + a.shape[1:], lambda b: (b,) + (0,) * (a.ndim - 1))
    osh = jax.ShapeDtypeStruct((B, NSA_GROUPS, R, HEAD_DIM), F32)
    osp = pl.BlockSpec((1, NSA_GROUPS, R, HEAD_DIM), lambda b: (b, 0, 0, 0))
    return pl.pallas_call(
        functools.partial(_dec_local_body, n_cmp=nsub - 1, n_blk=n_blk, n_top=min(N_SEL, n_blk), pos0=pos0,
                          n_hist=n_hist, n_new=n_new),
        grid=(B,),
        in_specs=[b4(q), b4(kcmp), b4(vcmp), b4(win_pad), b4(qm_bd), b4(mem),
                  pl.BlockSpec(slopes.shape, lambda b: (0, 0, 0))],
        out_specs=[osp, osp, pl.BlockSpec((1, R, HEAD_DIM), lambda b: (b, 0, 0)),
                   pl.BlockSpec((1, NSA_GROUPS, TPAD, lanes), lambda b: (b, 0, 0, 0))],
        out_shape=[osh, osh, jax.ShapeDtypeStruct((B, R, HEAD_DIM), F32),
                   jax.ShapeDtypeStruct((B, NSA_GROUPS, TPAD, lanes), F32)],
        compiler_params=_cparams("parallel"),
    )(q, kcmp, vcmp, win_pad, qm_bd, mem, slopes)


def _dec_paged_body(pt_ref, slc_ref, sb_ref, new_ref, wqs_ref, wqb_ref, selp_ref, seln_ref, slope_ref,
                    oslc_ref, osb_ref, m_ref, l_ref, acc_ref, r_ref, accb_ref, *, n_pages, pos0):
    step = pl.program_id(1)
    lane = _iota((1, 128), 1)
    pos = pos0 + lane % TPAD
    slope = slope_ref[...]
    later = jnp.where(_iota((QB, QB), 1) > _iota((QB, QB), 0), 1.0, 0.0).astype(BF16)
    krow = _iota((QB, 1), 0)

    @pl.when(step == 0)
    def _():
        m_ref[...] = jnp.full(m_ref.shape, NEG, F32)
        l_ref[...] = jnp.zeros_like(l_ref)
        acc_ref[...] = jnp.zeros_like(acc_ref)
        r_ref[...] = jnp.zeros_like(r_ref)
        accb_ref[...] = jnp.zeros_like(accb_ref)

    def chunk(kslc, vslc, ksb, vsb, k0, selmask):
        dist = pos - (k0 + krow)
        s = _dot(kslc.astype(BF16), wqs_ref[0]) * SCALE - slope * dist.astype(F32)
        mask = selmask & (dist >= 0)
        s = jnp.where(mask, s, NEG)
        m_old = m_ref[...]
        m_new = jnp.maximum(m_old, jnp.max(s, axis=0, keepdims=True))
        a = jnp.exp(m_old - m_new)
        p = jnp.where(mask, jnp.exp(s - m_new), 0.0)
        l_ref[...] = a * l_ref[...] + jnp.sum(p, axis=0, keepdims=True)
        m_ref[...] = m_new
        pb = p.astype(BF16)
        upd = jnp.zeros((HEAD_DIM, 128), F32)
        for g in range(NSA_GROUPS):
            pg = jnp.where(lane // (NSA_HPG * TPAD) == g, pb, jnp.zeros_like(pb))
            upd = upd + _tn(vslc[:, g * HEAD_DIM:(g + 1) * HEAD_DIM].astype(BF16), pg)
        acc_ref[...] = a * acc_ref[...] + upd
        z = _dot(ksb.astype(BF16), wqb_ref[0]) * SCALE
        before = dist > 0
        sp, ls = _softplus_pair(z)
        nlk = jnp.where(before, sp, 0.0)
        hi, mid, lo = _split3(nlk)
        suffix = _dot(later, hi) + _dot(later, mid) + _dot(later, lo)
        w = jnp.where(before, jnp.exp(ls - (r_ref[...] + suffix)), 0.0).astype(BF16)
        r_ref[...] += jnp.sum(nlk, axis=0, keepdims=True)
        upd = jnp.zeros((HEAD_DIM, 128), F32)
        for h in range(SB_HEADS):
            wh = jnp.where(lane // TPAD == h, w, jnp.zeros_like(w))
            upd = upd + _tn(vsb[:, h * HEAD_DIM:(h + 1) * HEAD_DIM].astype(BF16), wh)
        accb_ref[...] += upd

    @pl.when(step == 0)
    def _():
        new = new_ref[0]
        chunk(new[:, 2 * KVG:3 * KVG], new[:, 3 * KVG:4 * KVG], new[:, 4 * KVG:4 * KVG + SB_W],
              new[:, 4 * KVG + SB_W:], pos0, seln_ref[0, 0:1, :] > 0.5)

    @pl.when(step > 0)
    def _():
        page = n_pages - step
        slc = slc_ref[0]
        sb = sb_ref[0]
        selrow = jnp.where(krow < SEL_BLOCK, selp_ref[0, 0, 0:1, :], selp_ref[0, 0, 1:2, :])
        chunk(slc[:, :KVG], slc[:, KVG:], sb[:, :SB_W], sb[:, SB_W:], page * QB, selrow > 0.5)

    @pl.when(step == n_pages)
    def _():
        oslc_ref[0] = acc_ref[...] / jnp.maximum(l_ref[...], 1e-30)
        osb_ref[0] = accb_ref[...]


def _dec_paged(pages, table, new_page, wq_slc, wq_sb, sel_pages, sel_new, slope_lanes, pos0):
    B, P = table.shape
    pidx = lambda b, s, pt: pt[b, jnp.minimum(P - s, P - 1)]
    osp = pl.BlockSpec((1, HEAD_DIM, 128), lambda b, s, pt: (b, 0, 0))
    osh = jax.ShapeDtypeStruct((B, HEAD_DIM, 128), F32)
    vec = pltpu.VMEM((1, 128), F32)
    mat = pltpu.VMEM((HEAD_DIM, 128), F32)
    return pl.pallas_call(
        functools.partial(_dec_paged_body, n_pages=P, pos0=pos0),
        grid_spec=pltpu.PrefetchScalarGridSpec(
            num_scalar_prefetch=1, grid=(B, P + 1),
            in_specs=[pl.BlockSpec((1, QB, 2 * KVG), lambda b, s, pt: (pidx(b, s, pt), 0, 1)),
                      pl.BlockSpec((1, QB, 2 * SB_W), lambda b, s, pt: (pidx(b, s, pt), 0, 1)),
                      pl.BlockSpec((1, QB, KV_CH), lambda b, s, pt: (b, 0, 0)),
                      pl.BlockSpec((1, KVG, 128), lambda b, s, pt: (b, 0, 0)),
                      pl.BlockSpec((1, SB_W, 128), lambda b, s, pt: (b, 0, 0)),
                      pl.BlockSpec((1, 1, 8, 128), lambda b, s, pt: (b, jnp.minimum(P - s, P - 1), 0, 0)),
                      pl.BlockSpec((1, 8, 128), lambda b, s, pt: (b, 0, 0)),
                      pl.BlockSpec((1, 128), lambda b, s, pt: (0, 0))],
            out_specs=[osp, osp],
            scratch_shapes=[vec, vec, mat, vec, mat]),
        out_shape=[osh, osh],
        compiler_params=_cparams("parallel", "arbitrary"),
    )(table, pages, pages, new_page, wq_slc, wq_sb, sel_pages, sel_new, slope_lanes)


def _gate3_body(ga_ref, gb_ref, gc_ref, a_ref, b_ref, c_ref, o_ref):
    o_ref[...] = (ga_ref[...] * a_ref[...] + gb_ref[...] * b_ref[...] + gc_ref[...] * c_ref[...]).astype(o_ref.dtype)


def _gate3(gates, a, b, c):
    return pl.pallas_call(_gate3_body, out_shape=jax.ShapeDtypeStruct(a.shape, BF16))(*gates, a, b, c)


def _alibi_slopes(n):
    return jnp.exp2(-8.0 * jnp.arange(1, n + 1, dtype=F32) / n)


def _layer_weights(p):
    bf = lambda a: a.astype(BF16)
    w_in, b_in = p["w_in"], p["b_in"]
    seg = lambda a, b: (bf(w_in[:, a:b]), b_in[a:b])
    ngate = 3 * NSA_HEADS
    d_ff = p["w_down"].shape[0]
    pe_rows = lambda pe: pe.astype(F32)
    return dict(
        q=seg(0, OFF_KV), kv=seg(OFF_KV, OFF_WIN), win=seg(OFF_WIN, OFF_QSB), qsm=seg(OFF_QSB, OFF_GNSA),
        gn=(bf(jnp.pad(w_in[:, OFF_GNSA:OFF_GMERGE], ((0, 0), (0, 128 - ngate)))),
            jnp.pad(b_in[OFF_GNSA:OFF_GMERGE], (0, 128 - ngate))),
        gm=seg(OFF_GMERGE, w_in.shape[1]),
        w1k=bf(p["w_cmp_k1"]), w1v=bf(p["w_cmp_v1"]), pek=pe_rows(p["pe_cmp_k"]), pev=pe_rows(p["pe_cmp_v"]),
        w2k=bf(p["w_cmp_k2"]), w2v=bf(p["w_cmp_v2"]),
        w_br_nsa=bf(p["w_br_nsa"]), w_br_sb=bf(p["w_br_sb"]), w_br_mem=bf(p["w_br_mem"]), w_o=bf(p["w_o"]),
        ln1=(p["ln1_g"], p["ln1_b"]), ln2=(p["ln2_g"], p["ln2_b"]),
        up_a=(bf(p["w_up"][:, :d_ff]), p["b_up"][:d_ff], p["w_conv"][:, :d_ff], p["b_conv"][:d_ff]),
        up_g=(bf(p["w_up"][:, d_ff:]), p["b_up"][d_ff:], p["w_conv"][:, d_ff:], p["b_conv"][d_ff:]),
        w_down=bf(p["w_down"]), b_down=p["b_down"], d_ff=d_ff)


def _project(xb, w, tm):
    mm = functools.partial(_matmul, xb, tm=tm)
    (q,) = mm(*w["q"], [BF16])
    kv, kvb = mm(*w["kv"], [F32, BF16])
    win, winb = mm(*w["win"], [F32, BF16])
    (qsm,) = mm(*w["qsm"], [BF16])
    (gn,) = mm(*w["gn"], [F32], act="sigmoid")
    (gm,) = mm(*w["gm"], [F32], act="sigmoid")
    return q, kv, kvb, win, winb, qsm, gn, gm


def _post_mixer(x2d, o_nsa, o_sb, o_mem, gm, w, alpha, tm):
    merged = _merge(o_nsa, o_sb, o_mem, gm, w["w_br_nsa"], w["w_br_sb"], w["w_br_mem"], tm=tm)
    d = x2d.shape[1]
    return _matmul_ln(merged, w["w_o"], jnp.zeros((d,), F32), x2d, *w["ln1"], alpha, [F32, BF16], tm=min(tm, 512))


def _prompt_layer(x, mem_kv_b, w, alpha):
    B, T, D = x.shape
    M = B * T
    x2d = x.reshape(M, D)
    q, kv, kvb, win, winb, qsm, gn, gm = _project(x2d.astype(BF16), w, 1024)
    pages = kv.reshape(M // QB, QB, KV_CH)
    table = jnp.arange(M // QB, dtype=jnp.int32).reshape(B, T // QB)
    kcmp, vcmp = _compress(pages, table, w["w1k"], w["w1v"], w["pek"], w["pev"], w["w2k"], w["w2v"])
    slopes = jnp.repeat(_alibi_slopes(NSA_HEADS).reshape(NSA_GROUPS, NSA_HPG), QB, axis=1)[..., None]
    o_nsa = _nsa_seq(q.reshape(B, T, -1), kcmp, vcmp, kvb.reshape(B, T, -1), winb.reshape(B, T, -1),
                     gn.reshape(B, T, -1), slopes)
    o_sb, o_mem = _sbmem_seq(qsm.reshape(B, T, -1), kvb.reshape(B, T, -1), mem_kv_b)
    x1, x1b = _post_mixer(x2d, o_nsa.reshape(M, -1), o_sb.reshape(M, -1), o_mem.reshape(M, -1), gm, w, alpha, 1024)
    d_ff = w["d_ff"]
    zero_h = jnp.zeros((B, CONV_W - 1, d_ff), F32)
    act, st_a, st_g = _ffn_up_seq(x1b.reshape(B, T, D), w["up_a"][0], w["up_g"][0], w["up_a"][1], w["up_g"][1],
                                  w["up_a"][2], w["up_g"][2], w["up_a"][3], w["up_g"][3], zero_h, zero_h)
    (y,) = _matmul_ln(act.reshape(M, d_ff), w["w_down"], w["b_down"], x1, *w["ln2"], alpha, [F32])
    win3 = win.reshape(B, T, WIN_CH)
    keep = min(WINDOW, T)
    win_out = win3[:, T - keep:] if T >= WINDOW else jnp.concatenate(
        [jnp.zeros((B, WINDOW, WIN_CH), F32), win3], axis=1)[:, -keep:]
    return (y.reshape(B, T, D), kv.reshape(B, T, KV_CH), win_out, jnp.concatenate([st_a, st_g], axis=-1))


def _lanes_ght(a, width):
    a = a.reshape(a.shape[0], a.shape[1], -1)
    return jnp.pad(a, ((0, 0), (0, 0), (0, width - a.shape[-1])))


def _sample_layer(x, mem_kv, pages, table, win_buf, conv_buf, w, alpha):
    B, T, D = x.shape
    M = B * T
    P = table.shape[1]
    pos0 = P * QB
    n_hist = win_buf.shape[1]
    assert T <= TPAD and T >= 2
    x2d = x.reshape(M, D)
    q, kv, _, win, _, qsm, gn, gm = _project(x2d.astype(BF16), w, M)
    kcmp, vcmp = _compress(pages, table, w["w1k"], w["w1v"], w["pek"], w["pev"], w["w2k"], w["w2v"])
    n_blk = -(-(pos0 + T) // SEL_BLOCK)
    padt = lambda a, ax: jnp.pad(a, [(0, TPAD - T) if i == ax else (0, 0) for i in range(a.ndim)])

    q5 = padt(q.reshape(B, T, NSA_GROUPS, NSA_HPG, HEAD_DIM).transpose(0, 2, 3, 1, 4), 3)
    q_rows = q5.reshape(B, NSA_GROUPS, NSA_HPG * TPAD, HEAD_DIM)
    eye_g = jnp.eye(NSA_GROUPS, dtype=BF16)
    wq_slc = _lanes_ght((q5.transpose(0, 1, 4, 2, 3).reshape(B, NSA_GROUPS, HEAD_DIM, 1, NSA_HPG * TPAD)
                         * eye_g[None, :, None, :, None]).reshape(B, KVG, -1), 128)
    qsb = padt(qsm[:, :SB_W].reshape(B, T, SB_HEADS, HEAD_DIM).transpose(0, 2, 1, 3), 2)
    eye_h = jnp.eye(SB_HEADS, dtype=BF16)
    wq_sb = _lanes_ght((qsb.transpose(0, 1, 3, 2).reshape(B, SB_HEADS, HEAD_DIM, 1, TPAD)
                        * eye_h[None, :, None, :, None]).reshape(B, SB_W, -1), 128)
    qmem = padt(qsm[:, SB_W:].reshape(B, T, MEM_HEADS, HEAD_DIM).transpose(0, 2, 1, 3), 2)
    qm_bd = (qmem[:, :, :, None, :] * eye_h[None, :, None, :, None]).reshape(B, MEM_HEADS * TPAD, MEM_W)

    slopes = _alibi_slopes(NSA_HEADS).reshape(NSA_GROUPS, NSA_HPG)
    slope_rows = jnp.repeat(slopes, TPAD, axis=1)[..., None]
    slope_lanes = jnp.pad(jnp.repeat(slopes.reshape(-1), TPAD), (0, 128 - NSA_HEADS * TPAD)).reshape(1, 128)

    win3 = win.reshape(B, T, WIN_CH)
    win_all = jnp.concatenate([win_buf, win3], axis=1)
    wp = QB * (-(-(n_hist + T) // QB))
    win_pad = jnp.pad(win_all, ((0, 0), (0, wp - n_hist - T), (0, 0)))
    o_cmp, o_win, o_mem, sel = _dec_local(q_rows, kcmp, vcmp, win_pad, qm_bd, mem_kv, slope_rows, pos0, n_hist, T,
                                          n_blk)

    sel_t = sel[:, :, :, :n_blk]
    lanes_of = lambda a: jnp.broadcast_to(a[:, :, :, None, :], a.shape[:3] + (NSA_HPG, TPAD))
    per_page = sel_t[..., :2 * P].reshape(B, NSA_GROUPS, TPAD, P, 2).transpose(0, 3, 4, 1, 2)
    sel_pages = lanes_of(per_page.reshape(B, P * 2, NSA_GROUPS, TPAD)).reshape(B, P, 2, -1)
    sel_pages = jnp.pad(sel_pages, ((0, 0), (0, 0), (0, 6), (0, 128 - sel_pages.shape[-1])))
    last = sel_t[..., n_blk - 1] if n_blk > 2 * P else jnp.zeros(sel_t.shape[:3], F32)
    sel_new = lanes_of(last[:, None]).reshape(B, 1, -1)
    sel_new = jnp.pad(sel_new, ((0, 0), (0, 7), (0, 128 - sel_new.shape[-1])))

    new_page = jnp.pad(kv.reshape(B, T, KV_CH), ((0, 0), (0, QB - T), (0, 0)))
    oslc_t, osb_t = _dec_paged(pages, table, new_page, wq_slc, wq_sb, sel_pages, sel_new, slope_lanes, pos0)

    rows_of = lambda o: o.reshape(B, NSA_GROUPS, NSA_HPG, TPAD, HEAD_DIM)[:, :, :, :T].transpose(0, 3, 1, 2, 4) \
        .reshape(M, NSA_Q)
    o_slc = oslc_t[:, :, :NSA_HEADS * TPAD].reshape(B, HEAD_DIM, NSA_GROUPS, NSA_HPG, TPAD)[..., :T] \
        .transpose(0, 4, 2, 3, 1).reshape(M, NSA_Q)
    o_sb = osb_t[:, :, :SB_HEADS * TPAD].reshape(B, HEAD_DIM, SB_HEADS, TPAD)[..., :T] \
        .transpose(0, 3, 2, 1).reshape(M, SB_W)
    o_mem2 = o_mem.reshape(B, MEM_HEADS, TPAD, HEAD_DIM)[:, :, :T].transpose(0, 2, 1, 3).reshape(M, MEM_W)
    gexp = jnp.repeat(gn[:, :3 * NSA_HEADS].reshape(M, 3, NSA_HEADS), HEAD_DIM, axis=-1)
    o_nsa = _gate3([gexp[:, br] for br in range(3)], rows_of(o_cmp), o_slc, rows_of(o_win))

    x1, x1b = _post_mixer(x2d, o_nsa, o_sb.astype(BF16), o_mem2.astype(BF16), gm, w, alpha, M)
    d_ff = w["d_ff"]
    hist = lambda back: jnp.pad(conv_buf[:, CONV_W - 1 - back:], ((0, 0), (0, T - back), (0, 0))).reshape(M, 2 * d_ff)
    h1, h2 = hist(1), hist(2)
    act, u_a, u_g = _ffn_up_rows(x1b, w["up_a"][0], w["up_g"][0], w["up_a"][1], w["up_g"][1], w["up_a"][2],
                                 w["up_g"][2], w["up_a"][3], w["up_g"][3],
                                 h1[:, :d_ff], h2[:, :d_ff], h1[:, d_ff:], h2[:, d_ff:], T)
    (y,) = _matmul_ln(act, w["w_down"], w["b_down"], x1, *w["ln2"], alpha, [F32], tm=M)
    u = jnp.concatenate([u_a, u_g], axis=-1).reshape(B, T, 2 * d_ff)
    return (y.reshape(B, T, D), kv.reshape(B, T, KV_CH), win_all[:, -n_hist:], u[:, T - (CONV_W - 1):])


def kernel(x_prompt, x_sample, mem_prompt, cache_kv_pages, page_table, cache_win_kv, cache_mem_kv, state_ffn_conv, w_in, b_in, w_cmp_k1, pe_cmp_k, w_cmp_k2, w_cmp_v1, pe_cmp_v, w_cmp_v2, w_br_nsa, w_br_sb, w_br_mem, w_o, w_mem_kv, b_mem_kv, ln1_g, ln1_b, w_up, b_up, w_conv, b_conv, w_down, b_down, ln2_g, ln2_b):
    depth = w_in.shape[0]
    alpha = (2.0 * depth) ** 0.25
    params = dict(w_in=w_in, b_in=b_in, w_cmp_k1=w_cmp_k1, pe_cmp_k=pe_cmp_k, w_cmp_k2=w_cmp_k2, w_cmp_v1=w_cmp_v1,
                  pe_cmp_v=pe_cmp_v, w_cmp_v2=w_cmp_v2, w_br_nsa=w_br_nsa, w_br_sb=w_br_sb, w_br_mem=w_br_mem,
                  w_o=w_o, ln1_g=ln1_g, ln1_b=ln1_b, w_up=w_up, b_up=b_up, w_conv=w_conv, b_conv=b_conv,
                  w_down=w_down, b_down=b_down, ln2_g=ln2_g, ln2_b=ln2_b)
    hp, hs = x_prompt, x_sample
    outs = [[] for _ in range(7)]
    B, Mt, D = mem_prompt.shape
    at = lambda a, l: a.reshape(a.shape[1:]) if a.shape[0] == 1 else a[l]
    for l in range(depth):
        w = _layer_weights({k: at(v, l) for k, v in params.items()})
        mem_kv, mem_kv_b = _matmul(mem_prompt.reshape(B * Mt, D).astype(BF16), at(w_mem_kv, l).astype(BF16),
                                   at(b_mem_kv, l), [F32, BF16], tm=B * Mt)
        hp, kv_p, win_p, conv_p = _prompt_layer(hp, mem_kv_b.reshape(B, Mt, -1), w, alpha)
        hs, kv_s, win_s, conv_s = _sample_layer(hs, at(cache_mem_kv, l), at(cache_kv_pages, l), page_table,
                                                at(cache_win_kv, l), at(state_ffn_conv, l), w, alpha)
        for o, v in zip(outs, (kv_p, win_p, mem_kv.reshape(B, Mt, -1), conv_p, kv_s, win_s, conv_s)):
            o.append(v)
    return (hp, hs) + tuple(jnp.stack(o) for o in outs)
```

```python
import functools
import math

import jax
import jax.numpy as jnp
from jax import lax
from jax.experimental import pallas as pl
from jax.experimental.pallas import tpu as pltpu

F32 = jnp.float32
BF16 = jnp.bfloat16
NEG = -1e30

HEAD_DIM = 128
NSA_HEADS = 8
NSA_GROUPS = 2
NSA_HPG = NSA_HEADS // NSA_GROUPS
CMP_LEN = 32
CMP_STRIDE = 16
SEL_BLOCK = 64
N_SEL = 16
WINDOW = 512
SB_HEADS = 4
MEM_HEADS = 4
CONV_W = 3
QB = 128
TPAD = 8
LN_EPS = 1e-5
PAGES_PER_STEP = 8
SB_CUT = 88.0
VMEM_LIMIT = 56 * 1024 * 1024

NSA_Q = NSA_HEADS * HEAD_DIM
KVG = NSA_GROUPS * HEAD_DIM
SB_W = SB_HEADS * HEAD_DIM
MEM_W = MEM_HEADS * HEAD_DIM
KV_CH = 4 * KVG + 2 * SB_W
WIN_CH = 2 * KVG
OFF_KV = NSA_Q
OFF_WIN = OFF_KV + KV_CH
OFF_QSB = OFF_WIN + WIN_CH
OFF_GNSA = OFF_QSB + SB_W + MEM_W
OFF_GMERGE = OFF_GNSA + 3 * NSA_HEADS
SCALE = HEAD_DIM ** -0.5


def _cparams(*sem):
    return pltpu.CompilerParams(dimension_semantics=sem, vmem_limit_bytes=VMEM_LIMIT)


def _nt(a, b):
    return lax.dot_general(a, b, (((1,), (1,)), ((), ())), preferred_element_type=F32)


def _dot(a, b):
    return jnp.dot(a, b, preferred_element_type=F32)


def _iota(shape, dim):
    return lax.broadcasted_iota(jnp.int32, shape, dim)


def _softmax_rows(s, mask):
    s = jnp.where(mask, s, NEG)
    m = jnp.max(s, axis=-1, keepdims=True)
    e = jnp.where(mask, jnp.exp(s - m), 0.0)
    return e / jnp.maximum(jnp.sum(e, axis=-1, keepdims=True), 1e-30)


def _split3(x):
    hi = x.astype(BF16)
    r1 = x - hi.astype(F32)
    mid = r1.astype(BF16)
    lo = (r1 - mid.astype(F32)).astype(BF16)
    return hi, mid, lo


def _softplus_pair(z):
    lg = jnp.log1p(jnp.exp(-jnp.abs(z)))
    return jnp.maximum(z, 0.0) + lg, jnp.minimum(z, 0.0) - lg


def _topk_mask(score, lane, n_top):
    sel = jnp.zeros(score.shape, F32)
    for _ in range(n_top):
        mx = jnp.max(score, axis=-1, keepdims=True)
        idx = jnp.min(jnp.where(score == mx, lane, 1 << 20), axis=-1, keepdims=True)
        hit = lane == idx
        sel = jnp.where(hit & (mx >= 0.0), 1.0, sel)
        score = jnp.where(hit, -3.0, score)
    return sel


def _rank_select_t(score_t, n_blk, n_top):
    blk = _iota((score_t.shape[0], 1), 0)
    cnt = jnp.zeros(score_t.shape, jnp.int32)
    for j in range(n_blk):
        row = score_t[j:j + 1, :]
        beats = (row > score_t) | ((row == score_t) & (blk > j))
        cnt = cnt + jnp.where(beats, 1, 0)
    return jnp.where((cnt < n_top) & (score_t >= 0.0), 1.0, 0.0)


def _pages_per_step(n_pages):
    return next(n for n in (PAGES_PER_STEP, 4, 2, 1) if n_pages % n == 0)


def _block_scores(imp, pos, lane, n_blk):
    cur = pos // SEL_BLOCK
    forced = (lane == 0) | (lane == cur) | (lane == cur - 1)
    visible = lane * SEL_BLOCK <= pos
    score = jnp.where(forced, 1e9, jnp.where(visible, imp, -1.0))
    return jnp.where(lane < n_blk, score, -2.0)


def _overlap(nsub, lanes, n_cmp, n_blk, transposed=False):
    shape = (lanes, nsub) if transposed else (nsub, lanes)
    n = _iota(shape, 1 if transposed else 0)
    j = _iota(shape, 0 if transposed else 1)
    st = n * CMP_STRIDE
    ov = (st < (j + 1) * SEL_BLOCK) & (st + CMP_LEN > j * SEL_BLOCK) & (n < n_cmp) & (j < n_blk)
    return jnp.where(ov, 1.0, 0.0).astype(BF16)


def _mm_body(x_ref, w_ref, b_ref, *o_refs, act):
    r = _dot(x_ref[...], w_ref[...]) + b_ref[...]
    if act == "sigmoid":
        r = jax.nn.sigmoid(r)
    for o in o_refs:
        o[...] = r.astype(o.dtype)


def _matmul(x, w, b, out_dtypes, act=None, tm=1024, tn=512):
    M, K = x.shape
    N = w.shape[1]
    tm, tn = min(tm, M), min(tn, N)
    assert M % tm == 0 and N % tn == 0
    return pl.pallas_call(
        functools.partial(_mm_body, act=act),
        grid=(M // tm, N // tn),
        in_specs=[pl.BlockSpec((tm, K), lambda i, j: (i, 0)),
                  pl.BlockSpec((K, tn), lambda i, j: (0, j)),
                  pl.BlockSpec((1, tn), lambda i, j: (0, j))],
        out_specs=[pl.BlockSpec((tm, tn), lambda i, j: (i, j)) for _ in out_dtypes],
        out_shape=[jax.ShapeDtypeStruct((M, N), d) for d in out_dtypes],
        compiler_params=_cparams("parallel", "parallel"),
    )(x, w, b.reshape(1, N).astype(F32))


def _mm_ln_body(x_ref, w_ref, b_ref, res_ref, g_ref, be_ref, *rest, nk, alpha, n_out):
    o_refs, acc_ref = rest[:n_out], rest[n_out]
    k = pl.program_id(1)

    @pl.when(k == 0)
    def _():
        acc_ref[...] = jnp.zeros_like(acc_ref)

    acc_ref[...] += _dot(x_ref[...], w_ref[...])

    @pl.when(k == nk - 1)
    def _():
        v = alpha * res_ref[...] + (acc_ref[...] + b_ref[...])
        mu = jnp.mean(v, axis=-1, keepdims=True)
        d = v - mu
        var = jnp.mean(d * d, axis=-1, keepdims=True)
        y = d * lax.rsqrt(var + LN_EPS) * g_ref[...] + be_ref[...]
        for o in o_refs:
            o[...] = y.astype(o.dtype)


def _matmul_ln(x, w, b, res, g, be, alpha, out_dtypes, tm=512, tk=512):
    M, K = x.shape
    N = w.shape[1]
    tm, tk = min(tm, M), min(tk, K)
    assert M % tm == 0 and K % tk == 0
    nk = K // tk
    row = lambda a: a.reshape(1, N).astype(F32)
    return pl.pallas_call(
        functools.partial(_mm_ln_body, nk=nk, alpha=alpha, n_out=len(out_dtypes)),
        grid=(M // tm, nk),
        in_specs=[pl.BlockSpec((tm, tk), lambda i, k: (i, k)),
                  pl.BlockSpec((tk, N), lambda i, k: (k, 0)),
                  pl.BlockSpec((1, N), lambda i, k: (0, 0)),
                  pl.BlockSpec((tm, N), lambda i, k: (i, 0)),
                  pl.BlockSpec((1, N), lambda i, k: (0, 0)),
                  pl.BlockSpec((1, N), lambda i, k: (0, 0))],
        out_specs=[pl.BlockSpec((tm, N), lambda i, k: (i, 0)) for _ in out_dtypes],
        out_shape=[jax.ShapeDtypeStruct((M, N), d) for d in out_dtypes],
        scratch_shapes=[pltpu.VMEM((tm, N), F32)],
        compiler_params=_cparams("parallel", "arbitrary"),
    )(x, w, row(b), res, row(g), row(be))


def _merge_body(on_ref, os_ref, om_ref, g0_ref, g1_ref, g2_ref, wn_ref, ws_ref, wm_ref, o_ref):
    r = (g0_ref[...] * _dot(on_ref[...], wn_ref[...])
         + g1_ref[...] * _dot(os_ref[...], ws_ref[...])
         + g2_ref[...] * _dot(om_ref[...], wm_ref[...]))
    o_ref[...] = r.astype(o_ref.dtype)


def _merge(o_nsa, o_sb, o_mem, gm, w_nsa, w_sb, w_mem, tm=1024, tn=512):
    M = o_nsa.shape[0]
    D = w_nsa.shape[1]
    tm, tn = min(tm, M), min(tn, D)
    nj = D // tn
    xs = lambda a: pl.BlockSpec((tm, a.shape[1]), lambda i, j: (i, 0))
    ws = lambda a: pl.BlockSpec((a.shape[0], tn), lambda i, j: (0, j))
    gs = lambda c: pl.BlockSpec((tm, tn), lambda i, j: (i, c * nj + j))
    return pl.pallas_call(
        _merge_body,
        grid=(M // tm, nj),
        in_specs=[xs(o_nsa), xs(o_sb), xs(o_mem), gs(0), gs(1), gs(2), ws(w_nsa), ws(w_sb), ws(w_mem)],
        out_specs=pl.BlockSpec((tm, tn), lambda i, j: (i, j)),
        out_shape=jax.ShapeDtypeStruct((M, D), BF16),
        compiler_params=_cparams("parallel", "parallel"),
    )(o_nsa, o_sb, o_mem, gm, gm, gm, w_nsa, w_sb, w_mem)


def _conv_taps(u, p1, p2, wc_ref, bc_ref):
    return bc_ref[...] + wc_ref[0:1, :] * p2 + wc_ref[1:2, :] * p1 + wc_ref[2:3, :] * u


def _ffn_up_seq_body(x_ref, wa_ref, wg_ref, ba_ref, bg_ref, wca_ref, wcg_ref, bca_ref, bcg_ref, ha_ref, hg_ref,
                     act_ref, sa_ref, sg_ref, ca_ref, cg_ref, *, nt, tm):
    t = pl.program_id(2)

    @pl.when(t == 0)
    def _():
        ca_ref[0:2, :] = ha_ref[0]
        cg_ref[0:2, :] = hg_ref[0]

    x = x_ref[0]
    r = _iota((tm, 1), 0)

    def half(w_ref, b_ref, wc_ref, bc_ref, c_ref, s_ref):
        u = _dot(x, w_ref[...]) + b_ref[...]
        h0, h1 = c_ref[0:1, :], c_ref[1:2, :]
        p1 = jnp.where(r == 0, h1, pltpu.roll(u, 1, 0))
        p2 = jnp.where(r == 0, h0, jnp.where(r == 1, h1, pltpu.roll(u, 2, 0)))
        c = _conv_taps(u, p1, p2, wc_ref, bc_ref)
        c_ref[0:2, :] = u[tm - 2:tm, :]

        @pl.when(t == nt - 1)
        def _():
            s_ref[0] = u[tm - 2:tm, :]

        return c

    a = half(wa_ref, ba_ref, wca_ref, bca_ref, ca_ref, sa_ref)
    g = half(wg_ref, bg_ref, wcg_ref, bcg_ref, cg_ref, sg_ref)
    act_ref[0] = (a * jax.nn.gelu(g, approximate=True)).astype(act_ref.dtype)


def _ffn_up_seq(x, w_a, w_g, b_a, b_g, wc_a, wc_g, bc_a, bc_g, h_a, h_g, tm=1024, tn=512):
    B, T, D = x.shape
    Fh = w_a.shape[1]
    tm, tn = min(tm, T), min(tn, Fh)
    assert T % tm == 0 and Fh % tn == 0 and tm >= 8
    nt = T // tm
    wsp = pl.BlockSpec((D, tn), lambda j, b, t: (0, j))
    rsp = pl.BlockSpec((1, tn), lambda j, b, t: (0, j))
    csp = pl.BlockSpec((CONV_W, tn), lambda j, b, t: (0, j))
    hsp = pl.BlockSpec((1, 2, tn), lambda j, b, t: (b, 0, j))
    row = lambda a: a.reshape(1, Fh)
    return pl.pallas_call(
        functools.partial(_ffn_up_seq_body, nt=nt, tm=tm),
        grid=(Fh // tn, B, nt),
        in_specs=[pl.BlockSpec((1, tm, D), lambda j, b, t: (b, t, 0)), wsp, wsp, rsp, rsp, csp, csp, rsp, rsp, hsp, hsp],
        out_specs=[pl.BlockSpec((1, tm, tn), lambda j, b, t: (b, t, j)), hsp, hsp],
        out_shape=[jax.ShapeDtypeStruct((B, T, Fh), BF16),
                   jax.ShapeDtypeStruct((B, 2, Fh), F32), jax.ShapeDtypeStruct((B, 2, Fh), F32)],
        scratch_shapes=[pltpu.VMEM((8, tn), F32), pltpu.VMEM((8, tn), F32)],
        compiler_params=_cparams("parallel", "parallel", "arbitrary"),
    )(x, w_a, w_g, row(b_a), row(b_g), wc_a, wc_g, row(bc_a), row(bc_g), h_a, h_g)


def _ffn_up_rows_body(x_ref, wa_ref, wg_ref, ba_ref, bg_ref, wca_ref, wcg_ref, bca_ref, bcg_ref,
                      h1a_ref, h2a_ref, h1g_ref, h2g_ref, act_ref, ua_ref, ug_ref, *, period):
    x = x_ref[...]
    tin = _iota((x.shape[0], 1), 0) % period

    def half(w_ref, b_ref, wc_ref, bc_ref, h1_ref, h2_ref, u_ref):
        u = _dot(x, w_ref[...]) + b_ref[...]
        u_ref[...] = u
        p1 = jnp.where(tin >= 1, pltpu.roll(u, 1, 0), h1_ref[...])
        p2 = jnp.where(tin >= 2, pltpu.roll(u, 2, 0), h2_ref[...])
        return _conv_taps(u, p1, p2, wc_ref, bc_ref)

    a = half(wa_ref, ba_ref, wca_ref, bca_ref, h1a_ref, h2a_ref, ua_ref)
    g = half(wg_ref, bg_ref, wcg_ref, bcg_ref, h1g_ref, h2g_ref, ug_ref)
    act_ref[...] = (a * jax.nn.gelu(g, approximate=True)).astype(act_ref.dtype)


def _ffn_up_rows(x, w_a, w_g, b_a, b_g, wc_a, wc_g, bc_a, bc_g, h1a, h2a, h1g, h2g, period, tn=512):
    M, D = x.shape
    Fh = w_a.shape[1]
    tn = min(tn, Fh)
    wsp = pl.BlockSpec((D, tn), lambda j: (0, j))
    rsp = pl.BlockSpec((1, tn), lambda j: (0, j))
    csp = pl.BlockSpec((CONV_W, tn), lambda j: (0, j))
    msp = pl.BlockSpec((M, tn), lambda j: (0, j))
    row = lambda a: a.reshape(1, Fh)
    return pl.pallas_call(
        functools.partial(_ffn_up_rows_body, period=period),
        grid=(Fh // tn,),
        in_specs=[pl.BlockSpec((M, D), lambda j: (0, 0)), wsp, wsp, rsp, rsp, csp, csp, rsp, rsp, msp, msp, msp, msp],
        out_specs=[msp, msp, msp],
        out_shape=[jax.ShapeDtypeStruct((M, Fh), BF16), jax.ShapeDtypeStruct((M, Fh), F32),
                   jax.ShapeDtypeStruct((M, Fh), F32)],
        compiler_params=_cparams("parallel"),
    )(x, w_a, w_g, row(b_a), row(b_g), wc_a, wc_g, row(bc_a), row(bc_g), h1a, h2a, h1g, h2g)


def _compress_body(pt_ref, *refs, n_steps, nsub, npg):
    page_refs = refs[:npg]
    w1k_ref, w1v_ref, pek_ref, pev_ref, w2k_ref, w2v_ref, ok_ref, ov_ref, stage_ref, xs_ref = refs[npg:]
    st = pl.program_id(1)
    sub = QB // CMP_STRIDE
    ncb = 2 * KVG // 128
    for j, page_ref in enumerate(page_refs):
        for cb in range(ncb):
            stage_ref[j * ncb + cb] = page_ref[0, :, cb * 128:(cb + 1) * 128]
        row0 = pl.multiple_of((st * npg + j) * sub, sub)
        for s in range(CMP_STRIDE):
            for cb in range(ncb):
                rows = stage_ref[pl.ds(j * ncb + cb, 1), pl.ds(s, sub, stride=CMP_STRIDE), :]
                xs_ref[pl.ds(s, 1), pl.ds(row0, sub), cb * 128:(cb + 1) * 128] = rows

    @pl.when(st == n_steps - 1)
    def _():
        for kv, (w1_ref, pe_ref, w2_ref, o_ref) in enumerate(((w1k_ref, pek_ref, w2k_ref, ok_ref),
                                                              (w1v_ref, pev_ref, w2v_ref, ov_ref))):
            acc = [jnp.zeros((NSA_GROUPS * nsub, HEAD_DIM), F32) for _ in range(CMP_LEN // CMP_STRIDE)]
            for s in range(CMP_STRIDE):
                x = xs_ref[s, :, kv * KVG:(kv + 1) * KVG]
                xg = jnp.concatenate([x[:, g * HEAD_DIM:(g + 1) * HEAD_DIM] for g in range(NSA_GROUPS)], axis=0)
                for r in range(CMP_LEN // CMP_STRIDE):
                    i = r * CMP_STRIDE + s
                    acc[r] = acc[r] + _dot((xg + pe_ref[i:i + 1, :]).astype(BF16), w1_ref[i])
            hid = acc[0] + pltpu.roll(acc[1], NSA_GROUPS * nsub - 1, 0)
            out = _dot(jax.nn.gelu(hid, approximate=True).astype(BF16), w2_ref[...])
            for g in range(NSA_GROUPS):
                o_ref[0, g] = out[g * nsub:(g + 1) * nsub, :].astype(o_ref.dtype)


def _compress(pages, table, w1k, w1v, pek, pev, w2k, w2v):
    B, P = table.shape
    nsub = P * (QB // CMP_STRIDE)
    assert CMP_LEN == 2 * CMP_STRIDE
    npg = _pages_per_step(P)
    full = lambda a: pl.BlockSpec(a.shape, lambda b, s, pt: (0,) * a.ndim)
    osp = pl.BlockSpec((1, NSA_GROUPS, nsub, HEAD_DIM), lambda b, s, pt: (b, 0, 0, 0))
    osh = jax.ShapeDtypeStruct((B, NSA_GROUPS, nsub, HEAD_DIM), BF16)
    page = lambda j: pl.BlockSpec((1, QB, 2 * KVG), lambda b, s, pt: (pt[b, s * npg + j], 0, 0))
    return pl.pallas_call(
        functools.partial(_compress_body, n_steps=P // npg, nsub=nsub, npg=npg),
        grid_spec=pltpu.PrefetchScalarGridSpec(
            num_scalar_prefetch=1, grid=(B, P // npg),
            in_specs=[page(j) for j in range(npg)]
            + [full(w1k), full(w1v), full(pek), full(pev), full(w2k), full(w2v)],
            out_specs=[osp, osp],
            scratch_shapes=[pltpu.VMEM((npg * 2 * KVG // 128, QB, 128), F32),
                            pltpu.VMEM((CMP_STRIDE, nsub, 2 * KVG), F32)]),
        out_shape=[osh, osh],
        compiler_params=_cparams("parallel", "arbitrary"),
    )(table, *([pages] * npg), w1k, w1v, pek, pev, w2k, w2v)


def _nsa_seq_body(q_ref, kc_ref, vc_ref, ks_ref, vs_ref, kw_ref, vw_ref, gate_ref, slope_ref, o_ref,
                  s_ref, mx_ref, sum_ref, acc_ref, *, n_cmp, n_blk, n_top, unroll_slc, unroll_win):
    g = pl.program_id(1)
    i = pl.program_id(2)
    t0 = i * QB
    R = NSA_HPG * QB
    q = q_ref[0]
    qs = jnp.concatenate([q[:, h * HEAD_DIM:(h + 1) * HEAD_DIM] for h in range(NSA_HPG)], axis=0)
    rowt = _iota((R, 1), 0) % QB
    pos = t0 + rowt
    slope = slope_ref[0]
    lane = _iota((1, QB), 1)

    kc = kc_ref[0, 0]
    nsub = kc.shape[0]
    ncol = _iota((1, nsub), 1)
    dist_c = pos - (ncol * CMP_STRIDE + (CMP_LEN - 1))
    s = _nt(qs, kc) * SCALE - slope * dist_c.astype(F32)
    p_c = _softmax_rows(s, (dist_c >= 0) & (ncol < n_cmp)).astype(BF16)
    o_cmp = _dot(p_c, vc_ref[0, 0])
    imp4 = _nt(_overlap(nsub, QB, n_cmp, n_blk, transposed=True), p_c)
    imp_t = imp4[:, 0:QB]
    for h in range(1, NSA_HPG):
        imp_t = imp_t + imp4[:, h * QB:(h + 1) * QB]
    blk = _iota((QB, 1), 0)
    posq = t0 + lane
    cur = posq // SEL_BLOCK
    forced = (blk == 0) | (blk == cur) | (blk == cur - 1)
    score_t = jnp.where(forced, 1e9, jnp.where(blk * SEL_BLOCK <= posq, imp_t, -1.0))
    score_t = jnp.where(blk < n_blk, score_t, -2.0)
    nb = 8 * (-(-n_blk // 8))
    sel_t = _rank_select_t(score_t[0:nb], n_blk, n_top)
    if nb < QB:
        sel_t = jnp.concatenate([sel_t, jnp.zeros((QB - nb, QB), F32)], axis=0)
    sel = sel_t.T.astype(BF16)

    slope_b = jnp.broadcast_to(slope, (R, QB))
    rk = (rowt - lane).astype(F32)

    def attend(lo, hi, k_ref, v_ref, mask_fn, unroll):
        mx_ref[...] = jnp.full(mx_ref.shape, NEG, F32)
        g_lo, g_hi = lo // unroll, (hi + unroll - 1) // unroll

        def scores(cg, carry):
            for u in range(unroll):
                c = cg * unroll + u
                k0 = pl.multiple_of(c * QB, QB)
                dist = rk + (t0 - k0).astype(F32)
                sc = _nt(qs, k_ref[0, pl.ds(k0, QB), :]) * SCALE - slope_b * dist
                sc = jnp.where(mask_fn(c, dist), sc, NEG)
                s_ref[:, pl.ds(k0, QB)] = sc
                mx_ref[...] = jnp.maximum(mx_ref[...], sc)
            return carry

        lax.fori_loop(g_lo, g_hi, scores, 0)
        m = jnp.max(mx_ref[...], axis=-1, keepdims=True)
        mx_ref[...] = jnp.broadcast_to(m, (R, QB))
        sum_ref[...] = jnp.zeros_like(sum_ref)
        acc_ref[...] = jnp.zeros_like(acc_ref)

        def weigh(cg, carry):
            for u in range(unroll):
                k0 = pl.multiple_of((cg * unroll + u) * QB, QB)
                p = jnp.exp(s_ref[:, pl.ds(k0, QB)] - mx_ref[...])
                sum_ref[...] += p
                acc_ref[...] += _dot(p.astype(BF16), v_ref[0, pl.ds(k0, QB), :])
            return carry

        lax.fori_loop(g_lo, g_hi, weigh, 0)
        l = jnp.sum(sum_ref[...], axis=-1, keepdims=True)
        return acc_ref[...] / jnp.maximum(l, 1e-30)

    def slc_mask(c, dist):
        j = _iota((QB, QB), 0)
        k = _iota((QB, QB), 1)
        e = jnp.where(j == 2 * c + k // SEL_BLOCK, 1.0, 0.0).astype(BF16)
        mk = _dot(sel, e)
        mk = jnp.concatenate([mk] * NSA_HPG, axis=0)
        return (mk > 0.5) & (dist >= 0.0)

    o_slc = attend(0, i + 1, ks_ref, vs_ref, slc_mask, unroll_slc)
    o_win = attend(jnp.maximum(i - WINDOW // QB, 0), i + 1, kw_ref, vw_ref,
                   lambda c, dist: (dist >= 0.0) & (dist < float(WINDOW)), unroll_win)

    gate = gate_ref[0]
    outs = []
    for h in range(NSA_HPG):
        rows = slice(h * QB, (h + 1) * QB)
        o_h = jnp.zeros((QB, HEAD_DIM), F32)
        for br, o_br in enumerate((o_cmp, o_slc, o_win)):
            col = br * NSA_HEADS + g * NSA_HPG + h
            gcol = jnp.sum(jnp.where(lane == col, gate, 0.0), axis=-1, keepdims=True)
            o_h = o_h + gcol * o_br[rows]
        outs.append(o_h)
    o_ref[0] = jnp.concatenate(outs, axis=-1).astype(o_ref.dtype)


def _nsa_seq(q, kcmp, vcmp, kvb, winb, gates, slopes):
    B, T, _ = q.shape
    nq = T // QB
    nsub = kcmp.shape[2]
    n_blk = -(-T // SEL_BLOCK)
    assert T % QB == 0 and n_blk <= QB and WINDOW % QB == 0
    divisor = lambda n: next(u for u in range(n, 0, -1) if nq % u == 0)
    R = NSA_HPG * QB
    csp = pl.BlockSpec((1, 1, nsub, HEAD_DIM), lambda b, g, i: (b, g, 0, 0))
    col = lambda c0: pl.BlockSpec((1, T, HEAD_DIM), lambda b, g, i: (b, 0, c0 + g))
    tile = pltpu.VMEM((R, QB), F32)
    return pl.pallas_call(
        functools.partial(_nsa_seq_body, n_cmp=nsub - 1, n_blk=n_blk, n_top=min(N_SEL, n_blk),
                          unroll_slc=divisor(4), unroll_win=divisor(2)),
        grid=(B, NSA_GROUPS, nq),
        in_specs=[pl.BlockSpec((1, QB, NSA_HPG * HEAD_DIM), lambda b, g, i: (b, i, g)), csp, csp,
                  col(2 * NSA_GROUPS), col(3 * NSA_GROUPS), col(0), col(NSA_GROUPS),
                  pl.BlockSpec((1, QB, 128), lambda b, g, i: (b, i, 0)),
                  pl.BlockSpec((1, R, 1), lambda b, g, i: (g, 0, 0))],
        out_specs=pl.BlockSpec((1, QB, NSA_HPG * HEAD_DIM), lambda b, g, i: (b, i, g)),
        out_shape=jax.ShapeDtypeStruct((B, T, NSA_Q), BF16),
        scratch_shapes=[pltpu.VMEM((R, T), F32), tile, tile, tile],
        compiler_params=_cparams("parallel", "parallel", "arbitrary"),
    )(q, kcmp, vcmp, kvb, kvb, winb, winb, gates, slopes)


def _sb_chunk(z, before, r, upper):
    sp, ls = _softplus_pair(z)
    nlk = jnp.where(before, sp, 0.0)
    hi, mid, lo = _split3(nlk)
    later = _dot(hi, upper) + _dot(mid, upper) + _dot(lo, upper)
    a = jnp.where(before, jnp.exp(ls - (r + later)), 0.0)
    return a, jnp.sum(nlk, axis=-1, keepdims=True)


def _upper(n):
    return jnp.where(_iota((n, n), 0) > _iota((n, n), 1), 1.0, 0.0).astype(BF16)


def _sbmem_seq_body(qs_ref, qm_ref, k_ref, v_ref, km_ref, vm_ref, osb_ref, om_ref, r_ref, acc_ref):
    i = pl.program_id(1)
    t0 = i * QB
    pos = t0 + _iota((QB, 1), 0)
    lane = _iota((1, QB), 1)
    upper = _upper(QB)
    r_ref[...] = jnp.zeros_like(r_ref)
    acc_ref[...] = jnp.zeros_like(acc_ref)

    def body(carry):
        step, _ = carry
        k0 = pl.multiple_of((i - step) * QB, QB)
        before = (pos - (k0 + lane)) > 0
        rmin = None
        for h in range(SB_HEADS):
            cols = slice(h * HEAD_DIM, (h + 1) * HEAD_DIM)
            z = _nt(qs_ref[0, :, cols], k_ref[0, pl.ds(k0, QB), cols]) * SCALE
            a, mass = _sb_chunk(z, before, r_ref[h], upper)
            acc_ref[h] += _dot(a.astype(BF16), v_ref[0, pl.ds(k0, QB), cols])
            r_new = r_ref[h] + mass
            r_ref[h] = r_new
            hmin = jnp.min(r_new)
            rmin = hmin if rmin is None else jnp.minimum(rmin, hmin)
        return step + 1, rmin

    lax.while_loop(lambda c: (c[0] <= i) & (c[1] < SB_CUT), body, (jnp.int32(0), jnp.float32(0.0)))
    for h in range(SB_HEADS):
        cols = slice(h * HEAD_DIM, (h + 1) * HEAD_DIM)
        osb_ref[0, :, cols] = acc_ref[h].astype(osb_ref.dtype)
        s = _nt(qm_ref[0, :, cols], km_ref[0, :, cols]) * SCALE
        p = _softmax_rows(s, jnp.full(s.shape, True))
        om_ref[0, :, cols] = _dot(p.astype(BF16), vm_ref[0, :, cols]).astype(om_ref.dtype)


def _sbmem_seq(qsm, kvb, memb):
    B, T, _ = qsm.shape
    Mt = memb.shape[1]
    nq = T // QB
    assert SB_W == MEM_W and (4 * KVG) % SB_W == 0
    qsp = lambda c: pl.BlockSpec((1, QB, SB_W), lambda b, i: (b, i, c))
    ksp = lambda c: pl.BlockSpec((1, T, SB_W), lambda b, i: (b, 0, c))
    msp = lambda c: pl.BlockSpec((1, Mt, MEM_W), lambda b, i: (b, 0, c))
    osp = pl.BlockSpec((1, QB, SB_W), lambda b, i: (b, i, 0))
    sb0 = 4 * KVG // SB_W
    return pl.pallas_call(
        _sbmem_seq_body,
        grid=(B, nq),
        in_specs=[qsp(0), qsp(1), ksp(sb0), ksp(sb0 + 1), msp(0), msp(1)],
        out_specs=[osp, osp],
        out_shape=[jax.ShapeDtypeStruct((B, T, SB_W), BF16), jax.ShapeDtypeStruct((B, T, MEM_W), BF16)],
        scratch_shapes=[pltpu.VMEM((SB_HEADS, QB, 1), F32), pltpu.VMEM((SB_HEADS, QB, HEAD_DIM), F32)],
        compiler_params=_cparams("parallel", "arbitrary"),
    )(qsm, qsm, kvb, kvb, memb, memb)


def _dec_local_body(q_ref, kc_ref, vc_ref, win_ref, qm_ref, mem_ref, slope_ref, ocmp_ref, owin_ref, omem_ref, sel_ref,
                    *, n_cmp, n_blk, n_top, pos0, n_hist, n_new):
    R = NSA_HPG * TPAD
    row = _iota((R, 1), 0)
    pos = pos0 + row % TPAD
    nsub = kc_ref.shape[2]
    lanes = sel_ref.shape[3]
    lane = _iota((1, lanes), 1)
    ncol = _iota((1, nsub), 1)
    nwin = win_ref.shape[1]
    kidx = _iota((1, nwin), 1)
    kp = pos0 - n_hist + kidx
    ov = _overlap(nsub, lanes, n_cmp, n_blk)
    for g in range(NSA_GROUPS):
        q = q_ref[0, g]
        slope = slope_ref[g]
        dist_c = pos - (ncol * CMP_STRIDE + (CMP_LEN - 1))
        s = _nt(q, kc_ref[0, g]) * SCALE - slope * dist_c.astype(F32)
        p_c = _softmax_rows(s, (dist_c >= 0) & (ncol < n_cmp)).astype(BF16)
        ocmp_ref[0, g] = _dot(p_c, vc_ref[0, g])
        imp4 = _dot(p_c, ov)
        imp = imp4[0:TPAD]
        for h in range(1, NSA_HPG):
            imp = imp + imp4[h * TPAD:(h + 1) * TPAD]
        score = _block_scores(imp, pos0 + _iota((TPAD, 1), 0), lane, n_blk)
        sel_ref[0, g] = _topk_mask(score, lane, n_top)

        kw = win_ref[0, :, g * HEAD_DIM:(g + 1) * HEAD_DIM].astype(BF16)
        vw = win_ref[0, :, KVG + g * HEAD_DIM:KVG + (g + 1) * HEAD_DIM].astype(BF16)
        dist_w = pos - kp
        s = _nt(q, kw) * SCALE - slope * dist_w.astype(F32)
        ok = (kidx < n_hist + n_new) & (kp >= 0) & (dist_w >= 0) & (dist_w < WINDOW)
        owin_ref[0, g] = _dot(_softmax_rows(s, ok).astype(BF16), vw)

    km = mem_ref[0, :, :MEM_W].astype(BF16)
    vm = mem_ref[0, :, MEM_W:].astype(BF16)
    s = _nt(qm_ref[0], km) * SCALE
    o_all = _dot(_softmax_rows(s, jnp.full(s.shape, True)).astype(BF16), vm)
    hrow = _iota((MEM_HEADS * TPAD, 1), 0) // TPAD
    o = jnp.zeros((MEM_HEADS * TPAD, HEAD_DIM), F32)
    for h in range(MEM_HEADS):
        o = o + jnp.where(hrow == h, o_all[:, h * HEAD_DIM:(h + 1) * HEAD_DIM], 0.0)
    omem_ref[0] = o


def _dec_local(q, kcmp, vcmp, win_pad, qm_bd, mem, slopes, pos0, n_hist, n_new, n_blk):
    B = q.shape[0]
    R = NSA_HPG * TPAD
    nsub = kcmp.shape[2]
    lanes = 128 * (-(-n_blk // 128))
    b4 = lambda a: pl.BlockSpec((1,) + a.shape[1:], lambda b: (b,) + (0,) * (a.ndim - 1))
    osh = jax.ShapeDtypeStruct((B, NSA_GROUPS, R, HEAD_DIM), F32)
    osp = pl.BlockSpec((1, NSA_GROUPS, R, HEAD_DIM), lambda b: (b, 0, 0, 0))
    return pl.pallas_call(
        functools.partial(_dec_local_body, n_cmp=nsub - 1, n_blk=n_blk, n_top=min(N_SEL, n_blk), pos0=pos0,
                          n_hist=n_hist, n_new=n_new),
        grid=(B,),
        in_specs=[b4(q), b4(kcmp), b4(vcmp), b4(win_pad), b4(qm_bd), b4(mem),
                  pl.BlockSpec(slopes.shape, lambda b: (0, 0, 0))],
        out_specs=[osp, osp, pl.BlockSpec((1, R, HEAD_DIM), lambda b: (b, 0, 0)),
                   pl.BlockSpec((1, NSA_GROUPS, TPAD, lanes), lambda b: (b, 0, 0, 0))],
        out_shape=[osh, osh, jax.ShapeDtypeStruct((B, R, HEAD_DIM), F32),
                   jax.ShapeDtypeStruct((B, NSA_GROUPS, TPAD, lanes), F32)],
        compiler_params=_cparams("parallel"),
    )(q, kcmp, vcmp, win_pad, qm_bd, mem, slopes)


def _dec_slc_body(pt_ref, *refs, npg, n_steps, pos0):
    page_refs = refs[:npg]
    new_ref, q_ref, sel_ref, slope_ref, o_ref, m_ref, l_ref, acc_ref = refs[npg:]
    st = pl.program_id(1)
    R = NSA_HPG * TPAD
    nsel = sel_ref.shape[3]
    pos = pos0 + _iota((R, 1), 0) % TPAD
    lane = _iota((1, QB), 1)
    slc0 = 2 * KVG

    @pl.when(st == 0)
    def _():
        m_ref[...] = jnp.full(m_ref.shape, NEG, F32)
        l_ref[...] = jnp.zeros_like(l_ref)
        acc_ref[...] = jnp.zeros_like(acc_ref)

    def update(chunks):
        for g in range(NSA_GROUPS):
            cols = slice(g * HEAD_DIM, (g + 1) * HEAD_DIM)
            scored = []
            m_old = m_ref[g]
            m_new = m_old
            for k, _, k0, blk0 in chunks:
                dist = pos - (k0 + lane)
                sc = _nt(q_ref[0, g], k[:, cols].astype(BF16)) * SCALE - slope_ref[g] * dist.astype(F32)
                e = jnp.where(_iota((nsel, QB), 0) == blk0 + _iota((nsel, QB), 1) // SEL_BLOCK, 1.0, 0.0)
                mask = (_dot(sel_ref[0, g], e.astype(BF16)) > 0.5) & (dist >= 0)
                sc = jnp.where(mask, sc, NEG)
                m_new = jnp.maximum(m_new, jnp.max(sc, axis=-1, keepdims=True))
                scored.append((sc, mask))
            a = jnp.exp(m_old - m_new)
            l = a * l_ref[g]
            acc = a * acc_ref[g]
            for (sc, mask), (_, v, _, _) in zip(scored, chunks):
                p = jnp.where(mask, jnp.exp(sc - m_new), 0.0)
                l = l + jnp.sum(p, axis=-1, keepdims=True)
                acc = acc + _dot(p.astype(BF16), v[:, cols].astype(BF16))
            m_ref[g] = m_new
            l_ref[g] = l
            acc_ref[g] = acc

    @pl.when(st == 0)
    def _():
        new = new_ref[0]
        update([(new[:, slc0:slc0 + KVG], new[:, slc0 + KVG:slc0 + 2 * KVG], pos0, pos0 // SEL_BLOCK)])

    chunks = []
    for j, page_ref in enumerate(page_refs):
        page = st * npg + j
        chunks.append((page_ref[0, :, 0:KVG], page_ref[0, :, KVG:2 * KVG], page * QB, page * (QB // SEL_BLOCK)))
    update(chunks)

    @pl.when(st == n_steps - 1)
    def _():
        for g in range(NSA_GROUPS):
            o_ref[0, g] = acc_ref[g] / jnp.maximum(l_ref[g], 1e-30)


def _dec_slc(pages, table, new_page, q_rows, sel_rows, slope_rows, pos0):
    B, P = table.shape
    R = NSA_HPG * TPAD
    npg = _pages_per_step(P)
    b4 = lambda a: pl.BlockSpec((1,) + a.shape[1:], lambda b, s, pt: (b,) + (0,) * (a.ndim - 1))
    page = lambda j: pl.BlockSpec((1, QB, 2 * KVG), lambda b, s, pt: (pt[b, s * npg + j], 0, 1))
    osp = pl.BlockSpec((1, NSA_GROUPS, R, HEAD_DIM), lambda b, s, pt: (b, 0, 0, 0))
    return pl.pallas_call(
        functools.partial(_dec_slc_body, npg=npg, n_steps=P // npg, pos0=pos0),
        grid_spec=pltpu.PrefetchScalarGridSpec(
            num_scalar_prefetch=1, grid=(B, P // npg),
            in_specs=[page(j) for j in range(npg)]
            + [b4(new_page), b4(q_rows), b4(sel_rows), pl.BlockSpec(slope_rows.shape, lambda b, s, pt: (0, 0, 0))],
            out_specs=osp,
            scratch_shapes=[pltpu.VMEM((NSA_GROUPS, R, 1), F32), pltpu.VMEM((NSA_GROUPS, R, 1), F32),
                            pltpu.VMEM((NSA_GROUPS, R, HEAD_DIM), F32)]),
        out_shape=jax.ShapeDtypeStruct((B, NSA_GROUPS, R, HEAD_DIM), F32),
        compiler_params=_cparams("parallel", "arbitrary"),
    )(table, *([pages] * npg), new_page, q_rows, sel_rows, slope_rows)


def _dec_sb_body(pt_ref, pages_ref, new_ref, q_ref, o_ref, buf_ref, sem_ref, r_ref, acc_ref, *, n_pages, pos0, n_new):
    b = pl.program_id(0)
    R = SB_HEADS * TPAD
    row = _iota((R, 1), 0)
    pos = pos0 + row % TPAD
    live = row % TPAD < n_new
    lane = _iota((1, QB), 1)
    upper = _upper(QB)
    sb0 = 4 * KVG

    def page_copy(page, slot):
        return pltpu.make_async_copy(pages_ref.at[pt_ref[b, page], :, pl.ds(sb0, 2 * SB_W)], buf_ref.at[slot],
                                     sem_ref.at[slot])

    page_copy(n_pages - 1, 0).start()
    r_ref[...] = jnp.zeros_like(r_ref)
    acc_ref[...] = jnp.zeros_like(acc_ref)

    def chunk(k, v, k0):
        before = (pos - (k0 + lane)) > 0
        z = _nt(q_ref[0], k.astype(BF16)) * SCALE
        a, mass = _sb_chunk(z, before, r_ref[...], upper)
        acc_ref[...] += _dot(a.astype(BF16), v.astype(BF16))
        r_new = r_ref[...] + mass
        r_ref[...] = r_new
        return jnp.min(jnp.where(live, r_new, SB_CUT))

    new = new_ref[0]
    rmin0 = chunk(new[:, sb0:sb0 + SB_W], new[:, sb0 + SB_W:], pos0)

    def body(carry):
        j, _ = carry
        slot = j % 2
        page = n_pages - 1 - j
        page_copy(page, slot).wait()

        @pl.when(j + 1 < n_pages)
        def _():
            page_copy(page - 1, 1 - slot).start()

        rmin = chunk(buf_ref[slot, :, 0:SB_W], buf_ref[slot, :, SB_W:2 * SB_W], page * QB)
        return j + 1, rmin

    done, _ = lax.while_loop(lambda c: (c[0] < n_pages) & (c[1] < SB_CUT), body, (jnp.int32(0), rmin0))

    @pl.when(done < n_pages)
    def _():
        page_copy(n_pages - 1 - done, done % 2).wait()

    hrow = row // TPAD
    o = jnp.zeros((R, HEAD_DIM), F32)
    for h in range(SB_HEADS):
        o = o + jnp.where(hrow == h, acc_ref[:, h * HEAD_DIM:(h + 1) * HEAD_DIM], 0.0)
    o_ref[0] = o


def _dec_sb(pages, table, new_page, q_bd, pos0, n_new):
    B, P = table.shape
    R = SB_HEADS * TPAD
    return pl.pallas_call(
        functools.partial(_dec_sb_body, n_pages=P, pos0=pos0, n_new=n_new),
        grid_spec=pltpu.PrefetchScalarGridSpec(
            num_scalar_prefetch=1, grid=(B,),
            in_specs=[pl.BlockSpec(memory_space=pl.ANY),
                      pl.BlockSpec((1, QB, KV_CH), lambda b, pt: (b, 0, 0)),
                      pl.BlockSpec((1, R, SB_W), lambda b, pt: (b, 0, 0))],
            out_specs=pl.BlockSpec((1, R, HEAD_DIM), lambda b, pt: (b, 0, 0)),
            scratch_shapes=[pltpu.VMEM((2, QB, 2 * SB_W), F32), pltpu.SemaphoreType.DMA((2,)),
                            pltpu.VMEM((R, 1), F32), pltpu.VMEM((R, SB_W), F32)]),
        out_shape=jax.ShapeDtypeStruct((B, R, HEAD_DIM), F32),
        compiler_params=_cparams("arbitrary"),
    )(table, pages, new_page, q_bd)


def _gate3_body(ga_ref, gb_ref, gc_ref, a_ref, b_ref, c_ref, o_ref):
    o_ref[...] = (ga_ref[...] * a_ref[...] + gb_ref[...] * b_ref[...] + gc_ref[...] * c_ref[...]).astype(o_ref.dtype)


def _gate3(gates, a, b, c):
    return pl.pallas_call(_gate3_body, out_shape=jax.ShapeDtypeStruct(a.shape, BF16))(*gates, a, b, c)


def _alibi_slopes(n):
    return jnp.exp2(-8.0 * jnp.arange(1, n + 1, dtype=F32) / n)


def _layer_weights(p):
    bf = lambda a: a.astype(BF16)
    w_in, b_in = p["w_in"], p["b_in"]
    seg = lambda a, b: (bf(w_in[:, a:b]), b_in[a:b])
    ngate = 3 * NSA_HEADS
    d_ff = p["w_down"].shape[0]
    pe_rows = lambda pe: pe.astype(F32)
    return dict(
        q=seg(0, OFF_KV), kv=seg(OFF_KV, OFF_WIN), win=seg(OFF_WIN, OFF_QSB), qsm=seg(OFF_QSB, OFF_GNSA),
        gn=(bf(jnp.pad(w_in[:, OFF_GNSA:OFF_GMERGE], ((0, 0), (0, 128 - ngate)))),
            jnp.pad(b_in[OFF_GNSA:OFF_GMERGE], (0, 128 - ngate))),
        gm=seg(OFF_GMERGE, w_in.shape[1]),
        w1k=bf(p["w_cmp_k1"]), w1v=bf(p["w_cmp_v1"]), pek=pe_rows(p["pe_cmp_k"]), pev=pe_rows(p["pe_cmp_v"]),
        w2k=bf(p["w_cmp_k2"]), w2v=bf(p["w_cmp_v2"]),
        w_br_nsa=bf(p["w_br_nsa"]), w_br_sb=bf(p["w_br_sb"]), w_br_mem=bf(p["w_br_mem"]), w_o=bf(p["w_o"]),
        ln1=(p["ln1_g"], p["ln1_b"]), ln2=(p["ln2_g"], p["ln2_b"]),
        up_a=(bf(p["w_up"][:, :d_ff]), p["b_up"][:d_ff], p["w_conv"][:, :d_ff], p["b_conv"][:d_ff]),
        up_g=(bf(p["w_up"][:, d_ff:]), p["b_up"][d_ff:], p["w_conv"][:, d_ff:], p["b_conv"][d_ff:]),
        w_down=bf(p["w_down"]), b_down=p["b_down"], d_ff=d_ff)


def _project(xb, w, tm):
    mm = functools.partial(_matmul, xb, tm=tm)
    (q,) = mm(*w["q"], [BF16])
    kv, kvb = mm(*w["kv"], [F32, BF16])
    win, winb = mm(*w["win"], [F32, BF16])
    (qsm,) = mm(*w["qsm"], [BF16])
    (gn,) = mm(*w["gn"], [F32], act="sigmoid")
    (gm,) = mm(*w["gm"], [F32], act="sigmoid")
    return q, kv, kvb, win, winb, qsm, gn, gm


def _post_mixer(x2d, o_nsa, o_sb, o_mem, gm, w, alpha, tm):
    merged = _merge(o_nsa, o_sb, o_mem, gm, w["w_br_nsa"], w["w_br_sb"], w["w_br_mem"], tm=tm)
    d = x2d.shape[1]
    return _matmul_ln(merged, w["w_o"], jnp.zeros((d,), F32), x2d, *w["ln1"], alpha, [F32, BF16], tm=min(tm, 512))


def _prompt_layer(x, mem_kv_b, w, alpha):
    B, T, D = x.shape
    M = B * T
    x2d = x.reshape(M, D)
    q, kv, kvb, win, winb, qsm, gn, gm = _project(x2d.astype(BF16), w, 1024)
    pages = kv.reshape(M // QB, QB, KV_CH)
    table = jnp.arange(M // QB, dtype=jnp.int32).reshape(B, T // QB)
    kcmp, vcmp = _compress(pages, table, w["w1k"], w["w1v"], w["pek"], w["pev"], w["w2k"], w["w2v"])
    slopes = jnp.repeat(_alibi_slopes(NSA_HEADS).reshape(NSA_GROUPS, NSA_HPG), QB, axis=1)[..., None]
    o_nsa = _nsa_seq(q.reshape(B, T, -1), kcmp, vcmp, kvb.reshape(B, T, -1), winb.reshape(B, T, -1),
                     gn.reshape(B, T, -1), slopes)
    o_sb, o_mem = _sbmem_seq(qsm.reshape(B, T, -1), kvb.reshape(B, T, -1), mem_kv_b)
    x1, x1b = _post_mixer(x2d, o_nsa.reshape(M, -1), o_sb.reshape(M, -1), o_mem.reshape(M, -1), gm, w, alpha, 1024)
    d_ff = w["d_ff"]
    zero_h = jnp.zeros((B, CONV_W - 1, d_ff), F32)
    act, st_a, st_g = _ffn_up_seq(x1b.reshape(B, T, D), w["up_a"][0], w["up_g"][0], w["up_a"][1], w["up_g"][1],
                                  w["up_a"][2], w["up_g"][2], w["up_a"][3], w["up_g"][3], zero_h, zero_h)
    (y,) = _matmul_ln(act.reshape(M, d_ff), w["w_down"], w["b_down"], x1, *w["ln2"], alpha, [F32])
    win3 = win.reshape(B, T, WIN_CH)
    keep = min(WINDOW, T)
    win_out = win3[:, T - keep:] if T >= WINDOW else jnp.concatenate(
        [jnp.zeros((B, WINDOW, WIN_CH), F32), win3], axis=1)[:, -keep:]
    return (y.reshape(B, T, D), kv.reshape(B, T, KV_CH), win_out, jnp.concatenate([st_a, st_g], axis=-1))


def _sample_layer(x, mem_kv, pages, table, win_buf, conv_buf, w, alpha):
    B, T, D = x.shape
    M = B * T
    P = table.shape[1]
    pos0 = P * QB
    n_hist = win_buf.shape[1]
    assert T <= TPAD and T >= 2
    x2d = x.reshape(M, D)
    q, kv, _, win, _, qsm, gn, gm = _project(x2d.astype(BF16), w, M)
    kcmp, vcmp = _compress(pages, table, w["w1k"], w["w1v"], w["pek"], w["pev"], w["w2k"], w["w2v"])
    n_blk = -(-(pos0 + T) // SEL_BLOCK)
    padt = lambda a, ax: jnp.pad(a, [(0, TPAD - T) if i == ax else (0, 0) for i in range(a.ndim)])

    q5 = padt(q.reshape(B, T, NSA_GROUPS, NSA_HPG, HEAD_DIM).transpose(0, 2, 3, 1, 4), 3)
    q_rows = q5.reshape(B, NSA_GROUPS, NSA_HPG * TPAD, HEAD_DIM)
    eye_h = jnp.eye(SB_HEADS, dtype=BF16)

    def block_diag(qh):
        qh = padt(qh.reshape(B, T, SB_HEADS, HEAD_DIM).transpose(0, 2, 1, 3), 2)
        return (qh[:, :, :, None, :] * eye_h[None, :, None, :, None]).reshape(B, SB_HEADS * TPAD, SB_W)

    qsb_bd, qm_bd = block_diag(qsm[:, :SB_W]), block_diag(qsm[:, SB_W:])
    slope_rows = jnp.repeat(_alibi_slopes(NSA_HEADS).reshape(NSA_GROUPS, NSA_HPG), TPAD, axis=1)[..., None]

    win3 = win.reshape(B, T, WIN_CH)
    win_all = jnp.concatenate([win_buf, win3], axis=1)
    wp = QB * (-(-(n_hist + T) // QB))
    win_pad = jnp.pad(win_all, ((0, 0), (0, wp - n_hist - T), (0, 0)))
    o_cmp, o_win, o_mem, sel = _dec_local(q_rows, kcmp, vcmp, win_pad, qm_bd, mem_kv, slope_rows, pos0, n_hist, T,
                                          n_blk)

    sel_rows = jnp.broadcast_to(sel[:, :, None], (B, NSA_GROUPS, NSA_HPG) + sel.shape[2:]) \
        .reshape(B, NSA_GROUPS, NSA_HPG * TPAD, -1).astype(BF16)
    new_page = jnp.pad(kv.reshape(B, T, KV_CH), ((0, 0), (0, QB - T), (0, 0)))
    o_slc = _dec_slc(pages, table, new_page, q_rows, sel_rows, slope_rows, pos0)
    o_sb = _dec_sb(pages, table, new_page, qsb_bd, pos0, T)

    rows_of = lambda o: o.reshape(B, -1, TPAD, HEAD_DIM)[:, :, :T].transpose(0, 2, 1, 3).reshape(M, -1)
    gexp = jnp.repeat(gn[:, :3 * NSA_HEADS].reshape(M, 3, NSA_HEADS), HEAD_DIM, axis=-1)
    o_nsa = _gate3([gexp[:, br] for br in range(3)], rows_of(o_cmp), rows_of(o_slc), rows_of(o_win))

    x1, x1b = _post_mixer(x2d, o_nsa, rows_of(o_sb).astype(BF16), rows_of(o_mem).astype(BF16), gm, w, alpha, M)
    d_ff = w["d_ff"]
    hist = lambda back: jnp.pad(conv_buf[:, CONV_W - 1 - back:], ((0, 0), (0, T - back), (0, 0))).reshape(M, 2 * d_ff)
    h1, h2 = hist(1), hist(2)
    act, u_a, u_g = _ffn_up_rows(x1b, w["up_a"][0], w["up_g"][0], w["up_a"][1], w["up_g"][1], w["up_a"][2],
                                 w["up_g"][2], w["up_a"][3], w["up_g"][3],
                                 h1[:, :d_ff], h2[:, :d_ff], h1[:, d_ff:], h2[:, d_ff:], T)
    (y,) = _matmul_ln(act, w["w_down"], w["b_down"], x1, *w["ln2"], alpha, [F32], tm=M)
    u = jnp.concatenate([u_a, u_g], axis=-1).reshape(B, T, 2 * d_ff)
    return (y.reshape(B, T, D), kv.reshape(B, T, KV_CH), win_all[:, -n_hist:], u[:, T - (CONV_W - 1):])


def kernel(x_prompt, x_sample, mem_prompt, cache_kv_pages, page_table, cache_win_kv, cache_mem_kv, state_ffn_conv, w_in, b_in, w_cmp_k1, pe_cmp_k, w_cmp_k2, w_cmp_v1, pe_cmp_v, w_cmp_v2, w_br_nsa, w_br_sb, w_br_mem, w_o, w_mem_kv, b_mem_kv, ln1_g, ln1_b, w_up, b_up, w_conv, b_conv, w_down, b_down, ln2_g, ln2_b):
    depth = w_in.shape[0]
    alpha = (2.0 * depth) ** 0.25
    params = dict(w_in=w_in, b_in=b_in, w_cmp_k1=w_cmp_k1, pe_cmp_k=pe_cmp_k, w_cmp_k2=w_cmp_k2, w_cmp_v1=w_cmp_v1,
                  pe_cmp_v=pe_cmp_v, w_cmp_v2=w_cmp_v2, w_br_nsa=w_br_nsa, w_br_sb=w_br_sb, w_br_mem=w_br_mem,
                  w_o=w_o, ln1_g=ln1_g, ln1_b=ln1_b, w_up=w_up, b_up=b_up, w_conv=w_conv, b_conv=b_conv,
                  w_down=w_down, b_down=b_down, ln2_g=ln2_g, ln2_b=ln2_b)
    hp, hs = x_prompt, x_sample
    outs = [[] for _ in range(7)]
    B, Mt, D = mem_prompt.shape
    at = lambda a, l: a.reshape(a.shape[1:]) if a.shape[0] == 1 else a[l]
    for l in range(depth):
        w = _layer_weights({k: at(v, l) for k, v in params.items()})
        mem_kv, mem_kv_b = _matmul(mem_prompt.reshape(B * Mt, D).astype(BF16), at(w_mem_kv, l).astype(BF16),
                                   at(b_mem_kv, l), [F32, BF16], tm=B * Mt)
        hp, kv_p, win_p, conv_p = _prompt_layer(hp, mem_kv_b.reshape(B, Mt, -1), w, alpha)
        hs, kv_s, win_s, conv_s = _sample_layer(hs, at(cache_mem_kv, l), at(cache_kv_pages, l), page_table,
                                                at(cache_win_kv, l), at(state_ffn_conv, l), w, alpha)
        for o, v in zip(outs, (kv_p, win_p, mem_kv.reshape(B, Mt, -1), conv_p, kv_s, win_s, conv_s)):
            o.append(v)
    return (hp, hs) + tuple(jnp.stack(o) for o in outs)
```

```python
import functools
import math

import jax
import jax.numpy as jnp
from jax import lax
from jax.experimental import pallas as pl
from jax.experimental.pallas import tpu as pltpu

F32 = jnp.float32
BF16 = jnp.bfloat16
NEG = -1e30

HEAD_DIM = 128
NSA_HEADS = 8
NSA_GROUPS = 2
NSA_HPG = NSA_HEADS // NSA_GROUPS
CMP_LEN = 32
CMP_STRIDE = 16
SEL_BLOCK = 64
N_SEL = 16
WINDOW = 512
SB_HEADS = 4
MEM_HEADS = 4
CONV_W = 3
QB = 128
TPAD = 8
LN_EPS = 1e-5
PAGES_PER_STEP = 8
SLC_PAGES_PER_STEP = 16
SB_CUT = 88.0
VMEM_LIMIT = 56 * 1024 * 1024

NSA_Q = NSA_HEADS * HEAD_DIM
KVG = NSA_GROUPS * HEAD_DIM
SB_W = SB_HEADS * HEAD_DIM
MEM_W = MEM_HEADS * HEAD_DIM
KV_CH = 4 * KVG + 2 * SB_W
WIN_CH = 2 * KVG
OFF_KV = NSA_Q
OFF_WIN = OFF_KV + KV_CH
OFF_QSB = OFF_WIN + WIN_CH
OFF_GNSA = OFF_QSB + SB_W + MEM_W
OFF_GMERGE = OFF_GNSA + 3 * NSA_HEADS
SCALE = HEAD_DIM ** -0.5


def _cparams(*sem):
    return pltpu.CompilerParams(dimension_semantics=sem, vmem_limit_bytes=VMEM_LIMIT)


def _nt(a, b):
    return lax.dot_general(a, b, (((1,), (1,)), ((), ())), preferred_element_type=F32)


def _dot(a, b):
    return jnp.dot(a, b, preferred_element_type=F32)


def _iota(shape, dim):
    return lax.broadcasted_iota(jnp.int32, shape, dim)


def _softmax_rows(s, mask):
    s = jnp.where(mask, s, NEG)
    m = jnp.max(s, axis=-1, keepdims=True)
    e = jnp.where(mask, jnp.exp(s - m), 0.0)
    return e / jnp.maximum(jnp.sum(e, axis=-1, keepdims=True), 1e-30)


def _split3(x):
    hi = x.astype(BF16)
    r1 = x - hi.astype(F32)
    mid = r1.astype(BF16)
    lo = (r1 - mid.astype(F32)).astype(BF16)
    return hi, mid, lo


def _softplus_pair(z):
    lg = jnp.log1p(jnp.exp(-jnp.abs(z)))
    return jnp.maximum(z, 0.0) + lg, jnp.minimum(z, 0.0) - lg


def _rank_select(score, lane, n_blk, n_top):
    cnt = jnp.zeros(score.shape, jnp.int32)
    for j in range(n_blk):
        col = score[:, j:j + 1]
        beats = (col > score) | ((col == score) & (lane > j))
        cnt = cnt + jnp.where(beats, 1, 0)
    return jnp.where((cnt < n_top) & (score >= 0.0), 1.0, 0.0)


def _rank_select_t(score_t, n_blk, n_top):
    blk = _iota((score_t.shape[0], 1), 0)
    cnt = jnp.zeros(score_t.shape, jnp.int32)
    for j in range(n_blk):
        row = score_t[j:j + 1, :]
        beats = (row > score_t) | ((row == score_t) & (blk > j))
        cnt = cnt + jnp.where(beats, 1, 0)
    return jnp.where((cnt < n_top) & (score_t >= 0.0), 1.0, 0.0)


def _pages_per_step(n_pages, most=PAGES_PER_STEP):
    return next(n for n in range(most, 0, -1) if n_pages % n == 0)


def _block_scores(imp, pos, lane, n_blk):
    cur = pos // SEL_BLOCK
    forced = (lane == 0) | (lane == cur) | (lane == cur - 1)
    visible = lane * SEL_BLOCK <= pos
    score = jnp.where(forced, 1e9, jnp.where(visible, imp, -1.0))
    return jnp.where(lane < n_blk, score, -2.0)


def _overlap(nsub, lanes, n_cmp, n_blk, transposed=False):
    shape = (lanes, nsub) if transposed else (nsub, lanes)
    n = _iota(shape, 1 if transposed else 0)
    j = _iota(shape, 0 if transposed else 1)
    st = n * CMP_STRIDE
    ov = (st < (j + 1) * SEL_BLOCK) & (st + CMP_LEN > j * SEL_BLOCK) & (n < n_cmp) & (j < n_blk)
    return jnp.where(ov, 1.0, 0.0).astype(BF16)


def _mm_body(x_ref, w_ref, b_ref, *o_refs, act):
    r = _dot(x_ref[...], w_ref[...]) + b_ref[...]
    if act == "sigmoid":
        r = jax.nn.sigmoid(r)
    for o in o_refs:
        o[...] = r.astype(o.dtype)


def _matmul(x, w, b, out_dtypes, act=None, tm=1024, tn=512):
    M, K = x.shape
    N = w.shape[1]
    tm, tn = min(tm, M), min(tn, N)
    assert M % tm == 0 and N % tn == 0
    return pl.pallas_call(
        functools.partial(_mm_body, act=act),
        grid=(M // tm, N // tn),
        in_specs=[pl.BlockSpec((tm, K), lambda i, j: (i, 0)),
                  pl.BlockSpec((K, tn), lambda i, j: (0, j)),
                  pl.BlockSpec((1, tn), lambda i, j: (0, j))],
        out_specs=[pl.BlockSpec((tm, tn), lambda i, j: (i, j)) for _ in out_dtypes],
        out_shape=[jax.ShapeDtypeStruct((M, N), d) for d in out_dtypes],
        compiler_params=_cparams("parallel", "parallel"),
    )(x, w, b.reshape(1, N).astype(F32))


def _mm_ln_body(x_ref, w_ref, b_ref, res_ref, g_ref, be_ref, *rest, nk, alpha, n_out):
    o_refs, acc_ref = rest[:n_out], rest[n_out]
    k = pl.program_id(1)

    @pl.when(k == 0)
    def _():
        acc_ref[...] = jnp.zeros_like(acc_ref)

    acc_ref[...] += _dot(x_ref[...], w_ref[...])

    @pl.when(k == nk - 1)
    def _():
        v = alpha * res_ref[...] + (acc_ref[...] + b_ref[...])
        mu = jnp.mean(v, axis=-1, keepdims=True)
        d = v - mu
        var = jnp.mean(d * d, axis=-1, keepdims=True)
        y = d * lax.rsqrt(var + LN_EPS) * g_ref[...] + be_ref[...]
        for o in o_refs:
            o[...] = y.astype(o.dtype)


def _matmul_ln(x, w, b, res, g, be, alpha, out_dtypes, tm=512, tk=512):
    M, K = x.shape
    N = w.shape[1]
    tm, tk = min(tm, M), min(tk, K)
    assert M % tm == 0 and K % tk == 0
    nk = K // tk
    row = lambda a: a.reshape(1, N).astype(F32)
    return pl.pallas_call(
        functools.partial(_mm_ln_body, nk=nk, alpha=alpha, n_out=len(out_dtypes)),
        grid=(M // tm, nk),
        in_specs=[pl.BlockSpec((tm, tk), lambda i, k: (i, k)),
                  pl.BlockSpec((tk, N), lambda i, k: (k, 0)),
                  pl.BlockSpec((1, N), lambda i, k: (0, 0)),
                  pl.BlockSpec((tm, N), lambda i, k: (i, 0)),
                  pl.BlockSpec((1, N), lambda i, k: (0, 0)),
                  pl.BlockSpec((1, N), lambda i, k: (0, 0))],
        out_specs=[pl.BlockSpec((tm, N), lambda i, k: (i, 0)) for _ in out_dtypes],
        out_shape=[jax.ShapeDtypeStruct((M, N), d) for d in out_dtypes],
        scratch_shapes=[pltpu.VMEM((tm, N), F32)],
        compiler_params=_cparams("parallel", "arbitrary"),
    )(x, w, row(b), res, row(g), row(be))


def _merge_body(on_ref, os_ref, om_ref, g0_ref, g1_ref, g2_ref, wn_ref, ws_ref, wm_ref, o_ref):
    r = (g0_ref[...] * _dot(on_ref[...], wn_ref[...])
         + g1_ref[...] * _dot(os_ref[...], ws_ref[...])
         + g2_ref[...] * _dot(om_ref[...], wm_ref[...]))
    o_ref[...] = r.astype(o_ref.dtype)


def _merge(o_nsa, o_sb, o_mem, gm, w_nsa, w_sb, w_mem, tm=1024, tn=512):
    M = o_nsa.shape[0]
    D = w_nsa.shape[1]
    tm, tn = min(tm, M), min(tn, D)
    nj = D // tn
    xs = lambda a: pl.BlockSpec((tm, a.shape[1]), lambda i, j: (i, 0))
    ws = lambda a: pl.BlockSpec((a.shape[0], tn), lambda i, j: (0, j))
    gs = lambda c: pl.BlockSpec((tm, tn), lambda i, j: (i, c * nj + j))
    return pl.pallas_call(
        _merge_body,
        grid=(M // tm, nj),
        in_specs=[xs(o_nsa), xs(o_sb), xs(o_mem), gs(0), gs(1), gs(2), ws(w_nsa), ws(w_sb), ws(w_mem)],
        out_specs=pl.BlockSpec((tm, tn), lambda i, j: (i, j)),
        out_shape=jax.ShapeDtypeStruct((M, D), BF16),
        compiler_params=_cparams("parallel", "parallel"),
    )(o_nsa, o_sb, o_mem, gm, gm, gm, w_nsa, w_sb, w_mem)


def _conv_taps(u, p1, p2, wc_ref, bc_ref):
    return bc_ref[...] + wc_ref[0:1, :] * p2 + wc_ref[1:2, :] * p1 + wc_ref[2:3, :] * u


def _ffn_up_seq_body(x_ref, wa_ref, wg_ref, ba_ref, bg_ref, wca_ref, wcg_ref, bca_ref, bcg_ref, ha_ref, hg_ref,
                     act_ref, sa_ref, sg_ref, ca_ref, cg_ref, *, nt, tm):
    t = pl.program_id(2)

    @pl.when(t == 0)
    def _():
        ca_ref[0:2, :] = ha_ref[0]
        cg_ref[0:2, :] = hg_ref[0]

    x = x_ref[0]
    r = _iota((tm, 1), 0)

    def half(w_ref, b_ref, wc_ref, bc_ref, c_ref, s_ref):
        u = _dot(x, w_ref[...]) + b_ref[...]
        h0, h1 = c_ref[0:1, :], c_ref[1:2, :]
        p1 = jnp.where(r == 0, h1, pltpu.roll(u, 1, 0))
        p2 = jnp.where(r == 0, h0, jnp.where(r == 1, h1, pltpu.roll(u, 2, 0)))
        c = _conv_taps(u, p1, p2, wc_ref, bc_ref)
        c_ref[0:2, :] = u[tm - 2:tm, :]

        @pl.when(t == nt - 1)
        def _():
            s_ref[0] = u[tm - 2:tm, :]

        return c

    a = half(wa_ref, ba_ref, wca_ref, bca_ref, ca_ref, sa_ref)
    g = half(wg_ref, bg_ref, wcg_ref, bcg_ref, cg_ref, sg_ref)
    act_ref[0] = (a * jax.nn.gelu(g, approximate=True)).astype(act_ref.dtype)


def _ffn_up_seq(x, w_a, w_g, b_a, b_g, wc_a, wc_g, bc_a, bc_g, h_a, h_g, tm=1024, tn=512):
    B, T, D = x.shape
    Fh = w_a.shape[1]
    tm, tn = min(tm, T), min(tn, Fh)
    assert T % tm == 0 and Fh % tn == 0 and tm >= 8
    nt = T // tm
    wsp = pl.BlockSpec((D, tn), lambda j, b, t: (0, j))
    rsp = pl.BlockSpec((1, tn), lambda j, b, t: (0, j))
    csp = pl.BlockSpec((CONV_W, tn), lambda j, b, t: (0, j))
    hsp = pl.BlockSpec((1, 2, tn), lambda j, b, t: (b, 0, j))
    row = lambda a: a.reshape(1, Fh)
    return pl.pallas_call(
        functools.partial(_ffn_up_seq_body, nt=nt, tm=tm),
        grid=(Fh // tn, B, nt),
        in_specs=[pl.BlockSpec((1, tm, D), lambda j, b, t: (b, t, 0)), wsp, wsp, rsp, rsp, csp, csp, rsp, rsp, hsp, hsp],
        out_specs=[pl.BlockSpec((1, tm, tn), lambda j, b, t: (b, t, j)), hsp, hsp],
        out_shape=[jax.ShapeDtypeStruct((B, T, Fh), BF16),
                   jax.ShapeDtypeStruct((B, 2, Fh), F32), jax.ShapeDtypeStruct((B, 2, Fh), F32)],
        scratch_shapes=[pltpu.VMEM((8, tn), F32), pltpu.VMEM((8, tn), F32)],
        compiler_params=_cparams("parallel", "parallel", "arbitrary"),
    )(x, w_a, w_g, row(b_a), row(b_g), wc_a, wc_g, row(bc_a), row(bc_g), h_a, h_g)


def _ffn_up_rows_body(x_ref, wa_ref, wg_ref, ba_ref, bg_ref, wca_ref, wcg_ref, bca_ref, bcg_ref,
                      h1a_ref, h2a_ref, h1g_ref, h2g_ref, act_ref, ua_ref, ug_ref, *, period):
    x = x_ref[...]
    tin = _iota((x.shape[0], 1), 0) % period

    def half(w_ref, b_ref, wc_ref, bc_ref, h1_ref, h2_ref, u_ref):
        u = _dot(x, w_ref[...]) + b_ref[...]
        u_ref[...] = u
        p1 = jnp.where(tin >= 1, pltpu.roll(u, 1, 0), h1_ref[...])
        p2 = jnp.where(tin >= 2, pltpu.roll(u, 2, 0), h2_ref[...])
        return _conv_taps(u, p1, p2, wc_ref, bc_ref)

    a = half(wa_ref, ba_ref, wca_ref, bca_ref, h1a_ref, h2a_ref, ua_ref)
    g = half(wg_ref, bg_ref, wcg_ref, bcg_ref, h1g_ref, h2g_ref, ug_ref)
    act_ref[...] = (a * jax.nn.gelu(g, approximate=True)).astype(act_ref.dtype)


def _ffn_up_rows(x, w_a, w_g, b_a, b_g, wc_a, wc_g, bc_a, bc_g, h1a, h2a, h1g, h2g, period, tn=512):
    M, D = x.shape
    Fh = w_a.shape[1]
    tn = min(tn, Fh)
    wsp = pl.BlockSpec((D, tn), lambda j: (0, j))
    rsp = pl.BlockSpec((1, tn), lambda j: (0, j))
    csp = pl.BlockSpec((CONV_W, tn), lambda j: (0, j))
    msp = pl.BlockSpec((M, tn), lambda j: (0, j))
    row = lambda a: a.reshape(1, Fh)
    return pl.pallas_call(
        functools.partial(_ffn_up_rows_body, period=period),
        grid=(Fh // tn,),
        in_specs=[pl.BlockSpec((M, D), lambda j: (0, 0)), wsp, wsp, rsp, rsp, csp, csp, rsp, rsp, msp, msp, msp, msp],
        out_specs=[msp, msp, msp],
        out_shape=[jax.ShapeDtypeStruct((M, Fh), BF16), jax.ShapeDtypeStruct((M, Fh), F32),
                   jax.ShapeDtypeStruct((M, Fh), F32)],
        compiler_params=_cparams("parallel"),
    )(x, w_a, w_g, row(b_a), row(b_g), wc_a, wc_g, row(bc_a), row(bc_g), h1a, h2a, h1g, h2g)


def _compress_body(pt_ref, *refs, n_steps, nsub, npg):
    page_refs = refs[:npg]
    w1k_ref, w1v_ref, pe0_ref, pe1_ref, perm_ref, w2k_ref, w2v_ref, ok_ref, ov_ref, xs0_ref, xs1_ref = refs[npg:]
    st = pl.program_id(1)
    sub = QB // CMP_STRIDE
    for jp in range(npg // 2):
        x2 = jnp.concatenate([page_refs[2 * jp][0], page_refs[2 * jp + 1][0]], axis=0)
        row0 = pl.multiple_of((st * npg + 2 * jp) * sub, 2 * sub)
        for pe_ref, xs_ref in ((pe0_ref, xs0_ref), (pe1_ref, xs1_ref)):
            xp = _dot(perm_ref[...], (x2 + pe_ref[...]).astype(BF16)).astype(BF16)
            for s in range(CMP_STRIDE):
                xs_ref[s, pl.ds(row0, 2 * sub), :] = xp[s * 2 * sub:(s + 1) * 2 * sub, :]

    @pl.when(st == n_steps - 1)
    def _():
        for kv, (w1_ref, w2_ref, o_ref) in enumerate(((w1k_ref, w2k_ref, ok_ref), (w1v_ref, w2v_ref, ov_ref))):
            acc = [jnp.zeros((NSA_GROUPS * nsub, HEAD_DIM), F32) for _ in range(CMP_LEN // CMP_STRIDE)]
            for s in range(0, CMP_STRIDE, 2):
                for r, xs_ref in enumerate((xs0_ref, xs1_ref)):
                    xg = jnp.concatenate(
                        [jnp.concatenate([xs_ref[s + d, :, kv * KVG + g * HEAD_DIM:kv * KVG + (g + 1) * HEAD_DIM]
                                          for d in range(2)], axis=1) for g in range(NSA_GROUPS)], axis=0)
                    acc[r] = acc[r] + _dot(xg, w1_ref[(r * CMP_STRIDE + s) // 2])
            hid = acc[0] + pltpu.roll(acc[1], NSA_GROUPS * nsub - 1, 0)
            out = _dot(jax.nn.gelu(hid, approximate=True).astype(BF16), w2_ref[...])
            for g in range(NSA_GROUPS):
                o_ref[0, g] = out[g * nsub:(g + 1) * nsub, :].astype(o_ref.dtype)


def _compress(pages, table, w1k, w1v, pek, pev, w2k, w2v):
    B, P = table.shape
    nsub = P * (QB // CMP_STRIDE)
    assert CMP_LEN == 2 * CMP_STRIDE
    npg = _pages_per_step(P)
    assert npg % 2 == 0
    sub = QB // CMP_STRIDE

    def pe_tile(r):
        rows = lambda pe: jnp.tile(pe[r * CMP_STRIDE:(r + 1) * CMP_STRIDE], (2 * sub, 1))
        return jnp.concatenate([rows(pek)] * NSA_GROUPS + [rows(pev)] * NSA_GROUPS, axis=1)

    o = jnp.arange(2 * QB)
    src = ((o % (2 * sub)) // sub) * QB + CMP_STRIDE * (o % sub) + o // (2 * sub)
    perm = (src[:, None] == o[None, :]).astype(BF16)

    full = lambda a: pl.BlockSpec(a.shape, lambda b, s, pt: (0,) * a.ndim)
    osp = pl.BlockSpec((1, NSA_GROUPS, nsub, HEAD_DIM), lambda b, s, pt: (b, 0, 0, 0))
    osh = jax.ShapeDtypeStruct((B, NSA_GROUPS, nsub, HEAD_DIM), BF16)
    page = lambda j: pl.BlockSpec((1, QB, 2 * KVG), lambda b, s, pt: (pt[b, s * npg + j], 0, 0))
    pairs = lambda w1: w1.reshape(CMP_LEN // 2, 2 * HEAD_DIM, HEAD_DIM)
    consts = (pairs(w1k), pairs(w1v), pe_tile(0), pe_tile(1), perm, w2k, w2v)
    xs = pltpu.VMEM((CMP_STRIDE, nsub, 2 * KVG), BF16)
    return pl.pallas_call(
        functools.partial(_compress_body, n_steps=P // npg, nsub=nsub, npg=npg),
        grid_spec=pltpu.PrefetchScalarGridSpec(
            num_scalar_prefetch=1, grid=(B, P // npg),
            in_specs=[page(j) for j in range(npg)] + [full(a) for a in consts],
            out_specs=[osp, osp],
            scratch_shapes=[xs, xs]),
        out_shape=[osh, osh],
        compiler_params=_cparams("parallel", "arbitrary"),
    )(table, *([pages] * npg), *consts)


def _nsa_seq_body(q_ref, kc_ref, vc_ref, ks_ref, vs_ref, kw_ref, vw_ref, gate_ref, slope_ref, o_ref,
                  s_ref, mx_ref, sum_ref, acc_ref, *, n_cmp, n_blk, n_top, unroll_slc, unroll_win):
    g = pl.program_id(1)
    i = pl.program_id(2)
    t0 = i * QB
    R = NSA_HPG * QB
    q = q_ref[0]
    qs = jnp.concatenate([q[:, h * HEAD_DIM:(h + 1) * HEAD_DIM] for h in range(NSA_HPG)], axis=0)
    rowt = _iota((R, 1), 0) % QB
    pos = t0 + rowt
    slope = slope_ref[0]
    lane = _iota((1, QB), 1)

    kc = kc_ref[0, 0]
    nsub = kc.shape[0]
    ncol = _iota((1, nsub), 1)
    dist_c = pos - (ncol * CMP_STRIDE + (CMP_LEN - 1))
    s = _nt(qs, kc) * SCALE - slope * dist_c.astype(F32)
    p_c = _softmax_rows(s, (dist_c >= 0) & (ncol < n_cmp)).astype(BF16)
    o_cmp = _dot(p_c, vc_ref[0, 0])
    imp4 = _nt(_overlap(nsub, QB, n_cmp, n_blk, transposed=True), p_c)
    imp_t = imp4[:, 0:QB]
    for h in range(1, NSA_HPG):
        imp_t = imp_t + imp4[:, h * QB:(h + 1) * QB]
    blk = _iota((QB, 1), 0)
    posq = t0 + lane
    cur = posq // SEL_BLOCK
    forced = (blk == 0) | (blk == cur) | (blk == cur - 1)
    score_t = jnp.where(forced, 1e9, jnp.where(blk * SEL_BLOCK <= posq, imp_t, -1.0))
    score_t = jnp.where(blk < n_blk, score_t, -2.0)
    nb = 8 * (-(-n_blk // 8))
    sel_t = _rank_select_t(score_t[0:nb], n_blk, n_top)
    if nb < QB:
        sel_t = jnp.concatenate([sel_t, jnp.zeros((QB - nb, QB), F32)], axis=0)
    sel = sel_t.T.astype(BF16)

    slope_b = jnp.broadcast_to(slope, (R, QB))
    rk = (rowt - lane).astype(F32)

    def attend(lo, hi, k_ref, v_ref, mask_fn, unroll):
        mx_ref[...] = jnp.full(mx_ref.shape, NEG, F32)
        g_lo, g_hi = lo // unroll, (hi + unroll - 1) // unroll

        def scores(cg, carry):
            for u in range(unroll):
                c = cg * unroll + u
                k0 = pl.multiple_of(c * QB, QB)
                dist = rk + (t0 - k0).astype(F32)
                sc = _nt(qs, k_ref[0, pl.ds(k0, QB), :]) * SCALE - slope_b * dist
                sc = jnp.where(mask_fn(c, dist), sc, NEG)
                s_ref[:, pl.ds(k0, QB)] = sc
                mx_ref[...] = jnp.maximum(mx_ref[...], sc)
            return carry

        lax.fori_loop(g_lo, g_hi, scores, 0)
        m = jnp.max(mx_ref[...], axis=-1, keepdims=True)
        mx_ref[...] = jnp.broadcast_to(m, (R, QB))
        sum_ref[...] = jnp.zeros_like(sum_ref)
        acc_ref[...] = jnp.zeros_like(acc_ref)

        def weigh(cg, carry):
            for u in range(unroll):
                k0 = pl.multiple_of((cg * unroll + u) * QB, QB)
                p = jnp.exp(s_ref[:, pl.ds(k0, QB)] - mx_ref[...])
                sum_ref[...] += p
                acc_ref[...] += _dot(p.astype(BF16), v_ref[0, pl.ds(k0, QB), :])
            return carry

        lax.fori_loop(g_lo, g_hi, weigh, 0)
        l = jnp.sum(sum_ref[...], axis=-1, keepdims=True)
        return acc_ref[...] / jnp.maximum(l, 1e-30)

    def slc_mask(c, dist):
        j = _iota((QB, QB), 0)
        k = _iota((QB, QB), 1)
        e = jnp.where(j == 2 * c + k // SEL_BLOCK, 1.0, 0.0).astype(BF16)
        mk = _dot(sel, e)
        mk = jnp.concatenate([mk] * NSA_HPG, axis=0)
        return (mk > 0.5) & (dist >= 0.0)

    o_slc = attend(0, i + 1, ks_ref, vs_ref, slc_mask, unroll_slc)
    o_win = attend(jnp.maximum(i - WINDOW // QB, 0), i + 1, kw_ref, vw_ref,
                   lambda c, dist: (dist >= 0.0) & (dist < float(WINDOW)), unroll_win)

    gate = gate_ref[0]
    outs = []
    for h in range(NSA_HPG):
        rows = slice(h * QB, (h + 1) * QB)
        o_h = jnp.zeros((QB, HEAD_DIM), F32)
        for br, o_br in enumerate((o_cmp, o_slc, o_win)):
            col = br * NSA_HEADS + g * NSA_HPG + h
            gcol = jnp.sum(jnp.where(lane == col, gate, 0.0), axis=-1, keepdims=True)
            o_h = o_h + gcol * o_br[rows]
        outs.append(o_h)
    o_ref[0] = jnp.concatenate(outs, axis=-1).astype(o_ref.dtype)


def _nsa_seq(q, kcmp, vcmp, kvb, winb, gates, slopes):
    B, T, _ = q.shape
    nq = T // QB
    nsub = kcmp.shape[2]
    n_blk = -(-T // SEL_BLOCK)
    assert T % QB == 0 and n_blk <= QB and WINDOW % QB == 0
    divisor = lambda n: next(u for u in range(n, 0, -1) if nq % u == 0)
    R = NSA_HPG * QB
    csp = pl.BlockSpec((1, 1, nsub, HEAD_DIM), lambda b, g, i: (b, g, 0, 0))
    col = lambda c0: pl.BlockSpec((1, T, HEAD_DIM), lambda b, g, i: (b, 0, c0 + g))
    tile = pltpu.VMEM((R, QB), F32)
    return pl.pallas_call(
        functools.partial(_nsa_seq_body, n_cmp=nsub - 1, n_blk=n_blk, n_top=min(N_SEL, n_blk),
                          unroll_slc=divisor(4), unroll_win=divisor(2)),
        grid=(B, NSA_GROUPS, nq),
        in_specs=[pl.BlockSpec((1, QB, NSA_HPG * HEAD_DIM), lambda b, g, i: (b, i, g)), csp, csp,
                  col(2 * NSA_GROUPS), col(3 * NSA_GROUPS), col(0), col(NSA_GROUPS),
                  pl.BlockSpec((1, QB, 128), lambda b, g, i: (b, i, 0)),
                  pl.BlockSpec((1, R, 1), lambda b, g, i: (g, 0, 0))],
        out_specs=pl.BlockSpec((1, QB, NSA_HPG * HEAD_DIM), lambda b, g, i: (b, i, g)),
        out_shape=jax.ShapeDtypeStruct((B, T, NSA_Q), BF16),
        scratch_shapes=[pltpu.VMEM((R, T), F32), tile, tile, tile],
        compiler_params=_cparams("parallel", "parallel", "arbitrary"),
    )(q, kcmp, vcmp, kvb, kvb, winb, winb, gates, slopes)


def _sb_chunk(z, before, r, upper):
    sp, ls = _softplus_pair(z)
    nlk = jnp.where(before, sp, 0.0)
    hi, mid, lo = _split3(nlk)
    later = _dot(hi, upper) + _dot(mid, upper) + _dot(lo, upper)
    a = jnp.where(before, jnp.exp(ls - (r + later)), 0.0)
    return a, jnp.sum(nlk, axis=-1, keepdims=True)


def _upper(n):
    return jnp.where(_iota((n, n), 0) > _iota((n, n), 1), 1.0, 0.0).astype(BF16)


def _sbmem_seq_body(qs_ref, qm_ref, k_ref, v_ref, km_ref, vm_ref, osb_ref, om_ref, r_ref, acc_ref):
    i = pl.program_id(1)
    t0 = i * QB
    pos = t0 + _iota((QB, 1), 0)
    lane = _iota((1, QB), 1)
    upper = _upper(QB)
    r_ref[...] = jnp.zeros_like(r_ref)
    acc_ref[...] = jnp.zeros_like(acc_ref)

    def body(carry):
        step, _ = carry
        k0 = pl.multiple_of((i - step) * QB, QB)
        before = (pos - (k0 + lane)) > 0
        rmin = None
        for h in range(SB_HEADS):
            cols = slice(h * HEAD_DIM, (h + 1) * HEAD_DIM)
            z = _nt(qs_ref[0, :, cols], k_ref[0, pl.ds(k0, QB), cols]) * SCALE
            a, mass = _sb_chunk(z, before, r_ref[h], upper)
            acc_ref[h] += _dot(a.astype(BF16), v_ref[0, pl.ds(k0, QB), cols])
            r_new = r_ref[h] + mass
            r_ref[h] = r_new
            hmin = jnp.min(r_new)
            rmin = hmin if rmin is None else jnp.minimum(rmin, hmin)
        return step + 1, rmin

    lax.while_loop(lambda c: (c[0] <= i) & (c[1] < SB_CUT), body, (jnp.int32(0), jnp.float32(0.0)))
    for h in range(SB_HEADS):
        cols = slice(h * HEAD_DIM, (h + 1) * HEAD_DIM)
        osb_ref[0, :, cols] = acc_ref[h].astype(osb_ref.dtype)
        s = _nt(qm_ref[0, :, cols], km_ref[0, :, cols]) * SCALE
        p = _softmax_rows(s, jnp.full(s.shape, True))
        om_ref[0, :, cols] = _dot(p.astype(BF16), vm_ref[0, :, cols]).astype(om_ref.dtype)


def _sbmem_seq(qsm, kvb, memb):
    B, T, _ = qsm.shape
    Mt = memb.shape[1]
    nq = T // QB
    assert SB_W == MEM_W and (4 * KVG) % SB_W == 0
    qsp = lambda c: pl.BlockSpec((1, QB, SB_W), lambda b, i: (b, i, c))
    ksp = lambda c: pl.BlockSpec((1, T, SB_W), lambda b, i: (b, 0, c))
    msp = lambda c: pl.BlockSpec((1, Mt, MEM_W), lambda b, i: (b, 0, c))
    osp = pl.BlockSpec((1, QB, SB_W), lambda b, i: (b, i, 0))
    sb0 = 4 * KVG // SB_W
    return pl.pallas_call(
        _sbmem_seq_body,
        grid=(B, nq),
        in_specs=[qsp(0), qsp(1), ksp(sb0), ksp(sb0 + 1), msp(0), msp(1)],
        out_specs=[osp, osp],
        out_shape=[jax.ShapeDtypeStruct((B, T, SB_W), BF16), jax.ShapeDtypeStruct((B, T, MEM_W), BF16)],
        scratch_shapes=[pltpu.VMEM((SB_HEADS, QB, 1), F32), pltpu.VMEM((SB_HEADS, QB, HEAD_DIM), F32)],
        compiler_params=_cparams("parallel", "arbitrary"),
    )(qsm, qsm, kvb, kvb, memb, memb)


def _dec_local_body(q_ref, kc_ref, vc_ref, win_ref, qm_ref, mem_ref, slope_ref, ocmp_ref, owin_ref, omem_ref, sel_ref,
                    *, n_cmp, n_blk, n_top, pos0, n_hist, n_new):
    R = NSA_HPG * TPAD
    row = _iota((R, 1), 0)
    pos = pos0 + row % TPAD
    nsub = kc_ref.shape[2]
    lanes = sel_ref.shape[3]
    lane = _iota((1, lanes), 1)
    ncol = _iota((1, nsub), 1)
    nwin = win_ref.shape[1]
    kidx = _iota((1, nwin), 1)
    kp = pos0 - n_hist + kidx
    ov = _overlap(nsub, lanes, n_cmp, n_blk)
    for g in range(NSA_GROUPS):
        q = q_ref[0, g]
        slope = slope_ref[g]
        dist_c = pos - (ncol * CMP_STRIDE + (CMP_LEN - 1))
        s = _nt(q, kc_ref[0, g]) * SCALE - slope * dist_c.astype(F32)
        p_c = _softmax_rows(s, (dist_c >= 0) & (ncol < n_cmp)).astype(BF16)
        ocmp_ref[0, g] = _dot(p_c, vc_ref[0, g])
        imp4 = _dot(p_c, ov)
        imp = imp4[0:TPAD]
        for h in range(1, NSA_HPG):
            imp = imp + imp4[h * TPAD:(h + 1) * TPAD]
        score = _block_scores(imp, pos0 + _iota((TPAD, 1), 0), lane, n_blk)
        sel_ref[0, g] = _rank_select(score, lane, n_blk, n_top)

        kw = win_ref[0, :, g * HEAD_DIM:(g + 1) * HEAD_DIM].astype(BF16)
        vw = win_ref[0, :, KVG + g * HEAD_DIM:KVG + (g + 1) * HEAD_DIM].astype(BF16)
        dist_w = pos - kp
        s = _nt(q, kw) * SCALE - slope * dist_w.astype(F32)
        ok = (kidx < n_hist + n_new) & (kp >= 0) & (dist_w >= 0) & (dist_w < WINDOW)
        owin_ref[0, g] = _dot(_softmax_rows(s, ok).astype(BF16), vw)

    km = mem_ref[0, :, :MEM_W].astype(BF16)
    vm = mem_ref[0, :, MEM_W:].astype(BF16)
    s = _nt(qm_ref[0], km) * SCALE
    o_all = _dot(_softmax_rows(s, jnp.full(s.shape, True)).astype(BF16), vm)
    hrow = _iota((MEM_HEADS * TPAD, 1), 0) // TPAD
    o = jnp.zeros((MEM_HEADS * TPAD, HEAD_DIM), F32)
    for h in range(MEM_HEADS):
        o = o + jnp.where(hrow == h, o_all[:, h * HEAD_DIM:(h + 1) * HEAD_DIM], 0.0)
    omem_ref[0] = o


def _dec_local(q, kcmp, vcmp, win_pad, qm_bd, mem, slopes, pos0, n_hist, n_new, n_blk):
    B = q.shape[0]
    R = NSA_HPG * TPAD
    nsub = kcmp.shape[2]
    lanes = 128 * (-(-n_blk // 128))
    b4 = lambda a: pl.BlockSpec((1,) + a.shape[1:], lambda b: (b,) + (0,) * (a.ndim - 1))
    osh = jax.ShapeDtypeStruct((B, NSA_GROUPS, R, HEAD_DIM), F32)
    osp = pl.BlockSpec((1, NSA_GROUPS, R, HEAD_DIM), lambda b: (b, 0, 0, 0))
    return pl.pallas_call(
        functools.partial(_dec_local_body, n_cmp=nsub - 1, n_blk=n_blk, n_top=min(N_SEL, n_blk), pos0=pos0,
                          n_hist=n_hist, n_new=n_new),
        grid=(B,),
        in_specs=[b4(q), b4(kcmp), b4(vcmp), b4(win_pad), b4(qm_bd), b4(mem),
                  pl.BlockSpec(slopes.shape, lambda b: (0, 0, 0))],
        out_specs=[osp, osp, pl.BlockSpec((1, R, HEAD_DIM), lambda b: (b, 0, 0)),
                   pl.BlockSpec((1, NSA_GROUPS, TPAD, lanes), lambda b: (b, 0, 0, 0))],
        out_shape=[osh, osh, jax.ShapeDtypeStruct((B, R, HEAD_DIM), F32),
                   jax.ShapeDtypeStruct((B, NSA_GROUPS, TPAD, lanes), F32)],
        compiler_params=_cparams("parallel"),
    )(q, kcmp, vcmp, win_pad, qm_bd, mem, slopes)


def _dec_slc_body(pt_ref, *refs, npg, n_steps, pos0):
    page_refs = refs[:npg]
    new_ref, q_ref, sel_ref, seln_ref, e_ref, slope_ref, o_ref, m_ref, l_ref, acc_ref = refs[npg:]
    st = pl.program_id(1)
    R = NSA_HEADS * TPAD
    pos = pos0 + _iota((R, 1), 0) % TPAD
    slope = slope_ref[...]
    q = q_ref[0]
    slc0 = 2 * KVG

    def update(k, v, k0, chosen):
        dist = pos - (k0 + _iota((1, k.shape[0]), 1))
        mask = chosen & (dist >= 0)
        sc = jnp.where(mask, _nt(q, k) * SCALE - slope * dist.astype(F32), NEG)
        m_old = m_ref[...]
        m_new = jnp.maximum(m_old, jnp.max(sc, axis=-1, keepdims=True))
        a = jnp.exp(m_old - m_new)
        p = jnp.where(mask, jnp.exp(sc - m_new), 0.0)
        l_ref[...] = a * l_ref[...] + jnp.sum(p, axis=-1, keepdims=True)
        acc_ref[...] = a * acc_ref[...] + _dot(p.astype(BF16), v)
        m_ref[...] = m_new

    @pl.when(st == 0)
    def _():
        m_ref[...] = jnp.full(m_ref.shape, NEG, F32)
        l_ref[...] = jnp.zeros_like(l_ref)
        acc_ref[...] = jnp.zeros_like(acc_ref)
        new = new_ref[0]
        update(new[:, slc0:slc0 + KVG].astype(BF16), new[:, slc0 + KVG:slc0 + 2 * KVG].astype(BF16), pos0,
               seln_ref[0, :, 0:1] > 0.5)

    k = jnp.concatenate([p[0, :, 0:KVG].astype(BF16) for p in page_refs], axis=0)
    v = jnp.concatenate([p[0, :, KVG:2 * KVG].astype(BF16) for p in page_refs], axis=0)
    update(k, v, st * (npg * QB), _dot(sel_ref[0, 0], e_ref[...]) > 0.5)

    @pl.when(st == n_steps - 1)
    def _():
        o = acc_ref[...] / jnp.maximum(l_ref[...], 1e-30)
        rows = NSA_HPG * TPAD
        for g in range(NSA_GROUPS):
            o_ref[0, g] = o[g * rows:(g + 1) * rows, g * HEAD_DIM:(g + 1) * HEAD_DIM]


def _dec_slc(pages, table, new_page, q_rows, sel, slopes, pos0):
    B, P = table.shape
    rows = NSA_HPG * TPAD
    R = NSA_GROUPS * rows
    npg = _pages_per_step(P, SLC_PAGES_PER_STEP)
    n_steps = P // npg
    bps = npg * (QB // SEL_BLOCK)
    assert bps <= 128
    eye_g = jnp.eye(NSA_GROUPS, dtype=BF16)
    q_bd = (q_rows[:, :, :, None, :] * eye_g[None, :, None, :, None]).reshape(B, R, KVG)
    sel_rows = jnp.broadcast_to(sel[:, :, None], (B, NSA_GROUPS, NSA_HPG) + sel.shape[2:]).reshape(B, R, -1)
    lanes = lambda a: jnp.pad(a, [(0, 0)] * (a.ndim - 1) + [(0, 128 - a.shape[-1])]).astype(BF16)
    sel_steps = lanes(sel_rows[:, :, :n_steps * bps].reshape(B, R, n_steps, bps).transpose(0, 2, 1, 3))
    n_blk = sel.shape[-1]
    sel_new = lanes(sel_rows[:, :, n_steps * bps:n_steps * bps + 1]) if n_blk > n_steps * bps \
        else jnp.zeros((B, R, 128), BF16)
    expand = (_iota((128, npg * QB), 0) == _iota((128, npg * QB), 1) // SEL_BLOCK).astype(BF16)
    slope_rows = jnp.repeat(slopes.reshape(-1), TPAD)[:, None]

    b4 = lambda a: pl.BlockSpec((1,) + a.shape[1:], lambda b, s, pt: (b,) + (0,) * (a.ndim - 1))
    full = lambda a: pl.BlockSpec(a.shape, lambda b, s, pt: (0,) * a.ndim)
    page = lambda j: pl.BlockSpec((1, QB, 2 * KVG), lambda b, s, pt: (pt[b, s * npg + j], 0, 1))
    return pl.pallas_call(
        functools.partial(_dec_slc_body, npg=npg, n_steps=n_steps, pos0=pos0),
        grid_spec=pltpu.PrefetchScalarGridSpec(
            num_scalar_prefetch=1, grid=(B, n_steps),
            in_specs=[page(j) for j in range(npg)]
            + [b4(new_page), b4(q_bd), pl.BlockSpec((1, 1, R, 128), lambda b, s, pt: (b, s, 0, 0)), b4(sel_new),
               full(expand), full(slope_rows)],
            out_specs=pl.BlockSpec((1, NSA_GROUPS, rows, HEAD_DIM), lambda b, s, pt: (b, 0, 0, 0)),
            scratch_shapes=[pltpu.VMEM((R, 1), F32), pltpu.VMEM((R, 1), F32), pltpu.VMEM((R, KVG), F32)]),
        out_shape=jax.ShapeDtypeStruct((B, NSA_GROUPS, rows, HEAD_DIM), F32),
        compiler_params=_cparams("parallel", "arbitrary"),
    )(table, *([pages] * npg), new_page, q_bd, sel_steps, sel_new, expand, slope_rows)


def _dec_sb_body(pt_ref, pages_ref, new_ref, q_ref, o_ref, buf_ref, sem_ref, r_ref, acc_ref, *, n_pages, pos0, n_new):
    b = pl.program_id(0)
    R = SB_HEADS * TPAD
    row = _iota((R, 1), 0)
    pos = pos0 + row % TPAD
    live = row % TPAD < n_new
    lane = _iota((1, QB), 1)
    upper = _upper(QB)
    sb0 = 4 * KVG

    def page_copy(page, slot):
        return pltpu.make_async_copy(pages_ref.at[pt_ref[b, page], :, pl.ds(sb0, 2 * SB_W)], buf_ref.at[slot],
                                     sem_ref.at[slot])

    page_copy(n_pages - 1, 0).start()
    r_ref[...] = jnp.zeros_like(r_ref)
    acc_ref[...] = jnp.zeros_like(acc_ref)

    def chunk(k, v, k0):
        before = (pos - (k0 + lane)) > 0
        z = _nt(q_ref[0], k.astype(BF16)) * SCALE
        a, mass = _sb_chunk(z, before, r_ref[...], upper)
        acc_ref[...] += _dot(a.astype(BF16), v.astype(BF16))
        r_new = r_ref[...] + mass
        r_ref[...] = r_new
        return jnp.min(jnp.where(live, r_new, SB_CUT))

    new = new_ref[0]
    rmin0 = chunk(new[:, sb0:sb0 + SB_W], new[:, sb0 + SB_W:], pos0)

    def body(carry):
        j, _ = carry
        slot = j % 2
        page = n_pages - 1 - j
        page_copy(page, slot).wait()

        @pl.when(j + 1 < n_pages)
        def _():
            page_copy(page - 1, 1 - slot).start()

        rmin = chunk(buf_ref[slot, :, 0:SB_W], buf_ref[slot, :, SB_W:2 * SB_W], page * QB)
        return j + 1, rmin

    done, _ = lax.while_loop(lambda c: (c[0] < n_pages) & (c[1] < SB_CUT), body, (jnp.int32(0), rmin0))

    @pl.when(done < n_pages)
    def _():
        page_copy(n_pages - 1 - done, done % 2).wait()

    hrow = row // TPAD
    o = jnp.zeros((R, HEAD_DIM), F32)
    for h in range(SB_HEADS):
        o = o + jnp.where(hrow == h, acc_ref[:, h * HEAD_DIM:(h + 1) * HEAD_DIM], 0.0)
    o_ref[0] = o


def _dec_sb(pages, table, new_page, q_bd, pos0, n_new):
    B, P = table.shape
    R = SB_HEADS * TPAD
    return pl.pallas_call(
        functools.partial(_dec_sb_body, n_pages=P, pos0=pos0, n_new=n_new),
        grid_spec=pltpu.PrefetchScalarGridSpec(
            num_scalar_prefetch=1, grid=(B,),
            in_specs=[pl.BlockSpec(memory_space=pl.ANY),
                      pl.BlockSpec((1, QB, KV_CH), lambda b, pt: (b, 0, 0)),
                      pl.BlockSpec((1, R, SB_W), lambda b, pt: (b, 0, 0))],
            out_specs=pl.BlockSpec((1, R, HEAD_DIM), lambda b, pt: (b, 0, 0)),
            scratch_shapes=[pltpu.VMEM((2, QB, 2 * SB_W), F32), pltpu.SemaphoreType.DMA((2,)),
                            pltpu.VMEM((R, 1), F32), pltpu.VMEM((R, SB_W), F32)]),
        out_shape=jax.ShapeDtypeStruct((B, R, HEAD_DIM), F32),
        compiler_params=_cparams("arbitrary"),
    )(table, pages, new_page, q_bd)


def _gate3_body(ga_ref, gb_ref, gc_ref, a_ref, b_ref, c_ref, o_ref):
    o_ref[...] = (ga_ref[...] * a_ref[...] + gb_ref[...] * b_ref[...] + gc_ref[...] * c_ref[...]).astype(o_ref.dtype)


def _gate3(gates, a, b, c):
    return pl.pallas_call(_gate3_body, out_shape=jax.ShapeDtypeStruct(a.shape, BF16))(*gates, a, b, c)


def _alibi_slopes(n):
    return jnp.exp2(-8.0 * jnp.arange(1, n + 1, dtype=F32) / n)


def _layer_weights(p):
    bf = lambda a: a.astype(BF16)
    w_in, b_in = p["w_in"], p["b_in"]
    seg = lambda a, b: (bf(w_in[:, a:b]), b_in[a:b])
    ngate = 3 * NSA_HEADS
    d_ff = p["w_down"].shape[0]
    pe_rows = lambda pe: pe.astype(F32)
    return dict(
        q=seg(0, OFF_KV), kv=seg(OFF_KV, OFF_WIN), win=seg(OFF_WIN, OFF_QSB), qsm=seg(OFF_QSB, OFF_GNSA),
        gn=(bf(jnp.pad(w_in[:, OFF_GNSA:OFF_GMERGE], ((0, 0), (0, 128 - ngate)))),
            jnp.pad(b_in[OFF_GNSA:OFF_GMERGE], (0, 128 - ngate))),
        gm=seg(OFF_GMERGE, w_in.shape[1]),
        w1k=bf(p["w_cmp_k1"]), w1v=bf(p["w_cmp_v1"]), pek=pe_rows(p["pe_cmp_k"]), pev=pe_rows(p["pe_cmp_v"]),
        w2k=bf(p["w_cmp_k2"]), w2v=bf(p["w_cmp_v2"]),
        w_br_nsa=bf(p["w_br_nsa"]), w_br_sb=bf(p["w_br_sb"]), w_br_mem=bf(p["w_br_mem"]), w_o=bf(p["w_o"]),
        ln1=(p["ln1_g"], p["ln1_b"]), ln2=(p["ln2_g"], p["ln2_b"]),
        up_a=(bf(p["w_up"][:, :d_ff]), p["b_up"][:d_ff], p["w_conv"][:, :d_ff], p["b_conv"][:d_ff]),
        up_g=(bf(p["w_up"][:, d_ff:]), p["b_up"][d_ff:], p["w_conv"][:, d_ff:], p["b_conv"][d_ff:]),
        w_down=bf(p["w_down"]), b_down=p["b_down"], d_ff=d_ff)


def _project(xb, w, tm):
    mm = functools.partial(_matmul, xb, tm=tm)
    (q,) = mm(*w["q"], [BF16])
    kv, kvb = mm(*w["kv"], [F32, BF16])
    win, winb = mm(*w["win"], [F32, BF16])
    (qsm,) = mm(*w["qsm"], [BF16])
    (gn,) = mm(*w["gn"], [F32], act="sigmoid")
    (gm,) = mm(*w["gm"], [F32], act="sigmoid")
    return q, kv, kvb, win, winb, qsm, gn, gm


def _post_mixer(x2d, o_nsa, o_sb, o_mem, gm, w, alpha, tm):
    merged = _merge(o_nsa, o_sb, o_mem, gm, w["w_br_nsa"], w["w_br_sb"], w["w_br_mem"], tm=tm)
    d = x2d.shape[1]
    return _matmul_ln(merged, w["w_o"], jnp.zeros((d,), F32), x2d, *w["ln1"], alpha, [F32, BF16], tm=min(tm, 512))


def _prompt_layer(x, mem_kv_b, w, alpha):
    B, T, D = x.shape
    M = B * T
    x2d = x.reshape(M, D)
    q, kv, kvb, win, winb, qsm, gn, gm = _project(x2d.astype(BF16), w, 1024)
    pages = kv.reshape(M // QB, QB, KV_CH)
    table = jnp.arange(M // QB, dtype=jnp.int32).reshape(B, T // QB)
    kcmp, vcmp = _compress(pages, table, w["w1k"], w["w1v"], w["pek"], w["pev"], w["w2k"], w["w2v"])
    slopes = jnp.repeat(_alibi_slopes(NSA_HEADS).reshape(NSA_GROUPS, NSA_HPG), QB, axis=1)[..., None]
    o_nsa = _nsa_seq(q.reshape(B, T, -1), kcmp, vcmp, kvb.reshape(B, T, -1), winb.reshape(B, T, -1),
                     gn.reshape(B, T, -1), slopes)
    o_sb, o_mem = _sbmem_seq(qsm.reshape(B, T, -1), kvb.reshape(B, T, -1), mem_kv_b)
    x1, x1b = _post_mixer(x2d, o_nsa.reshape(M, -1), o_sb.reshape(M, -1), o_mem.reshape(M, -1), gm, w, alpha, 1024)
    d_ff = w["d_ff"]
    zero_h = jnp.zeros((B, CONV_W - 1, d_ff), F32)
    act, st_a, st_g = _ffn_up_seq(x1b.reshape(B, T, D), w["up_a"][0], w["up_g"][0], w["up_a"][1], w["up_g"][1],
                                  w["up_a"][2], w["up_g"][2], w["up_a"][3], w["up_g"][3], zero_h, zero_h)
    (y,) = _matmul_ln(act.reshape(M, d_ff), w["w_down"], w["b_down"], x1, *w["ln2"], alpha, [F32])
    win3 = win.reshape(B, T, WIN_CH)
    keep = min(WINDOW, T)
    win_out = win3[:, T - keep:] if T >= WINDOW else jnp.concatenate(
        [jnp.zeros((B, WINDOW, WIN_CH), F32), win3], axis=1)[:, -keep:]
    return (y.reshape(B, T, D), kv.reshape(B, T, KV_CH), win_out, jnp.concatenate([st_a, st_g], axis=-1))


def _sample_layer(x, mem_kv, pages, table, win_buf, conv_buf, w, alpha):
    B, T, D = x.shape
    M = B * T
    P = table.shape[1]
    pos0 = P * QB
    n_hist = win_buf.shape[1]
    assert T <= TPAD and T >= 2
    x2d = x.reshape(M, D)
    q, kv, _, win, _, qsm, gn, gm = _project(x2d.astype(BF16), w, M)
    kcmp, vcmp = _compress(pages, table, w["w1k"], w["w1v"], w["pek"], w["pev"], w["w2k"], w["w2v"])
    n_blk = -(-(pos0 + T) // SEL_BLOCK)
    padt = lambda a, ax: jnp.pad(a, [(0, TPAD - T) if i == ax else (0, 0) for i in range(a.ndim)])

    q5 = padt(q.reshape(B, T, NSA_GROUPS, NSA_HPG, HEAD_DIM).transpose(0, 2, 3, 1, 4), 3)
    q_rows = q5.reshape(B, NSA_GROUPS, NSA_HPG * TPAD, HEAD_DIM)
    eye_h = jnp.eye(SB_HEADS, dtype=BF16)

    def block_diag(qh):
        qh = padt(qh.reshape(B, T, SB_HEADS, HEAD_DIM).transpose(0, 2, 1, 3), 2)
        return (qh[:, :, :, None, :] * eye_h[None, :, None, :, None]).reshape(B, SB_HEADS * TPAD, SB_W)

    qsb_bd, qm_bd = block_diag(qsm[:, :SB_W]), block_diag(qsm[:, SB_W:])
    slopes = _alibi_slopes(NSA_HEADS).reshape(NSA_GROUPS, NSA_HPG)
    slope_rows = jnp.repeat(slopes, TPAD, axis=1)[..., None]

    win3 = win.reshape(B, T, WIN_CH)
    win_all = jnp.concatenate([win_buf, win3], axis=1)
    wp = QB * (-(-(n_hist + T) // QB))
    win_pad = jnp.pad(win_all, ((0, 0), (0, wp - n_hist - T), (0, 0)))
    o_cmp, o_win, o_mem, sel = _dec_local(q_rows, kcmp, vcmp, win_pad, qm_bd, mem_kv, slope_rows, pos0, n_hist, T,
                                          n_blk)

    new_page = jnp.pad(kv.reshape(B, T, KV_CH), ((0, 0), (0, QB - T), (0, 0)))
    o_slc = _dec_slc(pages, table, new_page, q_rows, sel[..., :n_blk], slopes, pos0)
    o_sb = _dec_sb(pages, table, new_page, qsb_bd, pos0, T)

    rows_of = lambda o: o.reshape(B, -1, TPAD, HEAD_DIM)[:, :, :T].transpose(0, 2, 1, 3).reshape(M, -1)
    gexp = jnp.repeat(gn[:, :3 * NSA_HEADS].reshape(M, 3, NSA_HEADS), HEAD_DIM, axis=-1)
    o_nsa = _gate3([gexp[:, br] for br in range(3)], rows_of(o_cmp), rows_of(o_slc), rows_of(o_win))

    x1, x1b = _post_mixer(x2d, o_nsa, rows_of(o_sb).astype(BF16), rows_of(o_mem).astype(BF16), gm, w, alpha, M)
    d_ff = w["d_ff"]
    hist = lambda back: jnp.pad(conv_buf[:, CONV_W - 1 - back:], ((0, 0), (0, T - back), (0, 0))).reshape(M, 2 * d_ff)
    h1, h2 = hist(1), hist(2)
    act, u_a, u_g = _ffn_up_rows(x1b, w["up_a"][0], w["up_g"][0], w["up_a"][1], w["up_g"][1], w["up_a"][2],
                                 w["up_g"][2], w["up_a"][3], w["up_g"][3],
                                 h1[:, :d_ff], h2[:, :d_ff], h1[:, d_ff:], h2[:, d_ff:], T)
    (y,) = _matmul_ln(act, w["w_down"], w["b_down"], x1, *w["ln2"], alpha, [F32], tm=M)
    u = jnp.concatenate([u_a, u_g], axis=-1).reshape(B, T, 2 * d_ff)
    return (y.reshape(B, T, D), kv.reshape(B, T, KV_CH), win_all[:, -n_hist:], u[:, T - (CONV_W - 1):])


def kernel(x_prompt, x_sample, mem_prompt, cache_kv_pages, page_table, cache_win_kv, cache_mem_kv, state_ffn_conv, w_in, b_in, w_cmp_k1, pe_cmp_k, w_cmp_k2, w_cmp_v1, pe_cmp_v, w_cmp_v2, w_br_nsa, w_br_sb, w_br_mem, w_o, w_mem_kv, b_mem_kv, ln1_g, ln1_b, w_up, b_up, w_conv, b_conv, w_down, b_down, ln2_g, ln2_b):
    depth = w_in.shape[0]
    alpha = (2.0 * depth) ** 0.25
    params = dict(w_in=w_in, b_in=b_in, w_cmp_k1=w_cmp_k1, pe_cmp_k=pe_cmp_k, w_cmp_k2=w_cmp_k2, w_cmp_v1=w_cmp_v1,
                  pe_cmp_v=pe_cmp_v, w_cmp_v2=w_cmp_v2, w_br_nsa=w_br_nsa, w_br_sb=w_br_sb, w_br_mem=w_br_mem,
                  w_o=w_o, ln1_g=ln1_g, ln1_b=ln1_b, w_up=w_up, b_up=b_up, w_conv=w_conv, b_conv=b_conv,
                  w_down=w_down, b_down=b_down, ln2_g=ln2_g, ln2_b=ln2_b)
    hp, hs = x_prompt, x_sample
    outs = [[] for _ in range(7)]
    B, Mt, D = mem_prompt.shape
    at = lambda a, l: a.reshape(a.shape[1:]) if a.shape[0] == 1 else a[l]
    for l in range(depth):
        w = _layer_weights({k: at(v, l) for k, v in params.items()})
        mem_kv, mem_kv_b = _matmul(mem_prompt.reshape(B * Mt, D).astype(BF16), at(w_mem_kv, l).astype(BF16),
                                   at(b_mem_kv, l), [F32, BF16], tm=B * Mt)
        hp, kv_p, win_p, conv_p = _prompt_layer(hp, mem_kv_b.reshape(B, Mt, -1), w, alpha)
        hs, kv_s, win_s, conv_s = _sample_layer(hs, at(cache_mem_kv, l), at(cache_kv_pages, l), page_table,
                                                at(cache_win_kv, l), at(state_ffn_conv, l), w, alpha)
        for o, v in zip(outs, (kv_p, win_p, mem_kv.reshape(B, Mt, -1), conv_p, kv_s, win_s, conv_s)):
            o.append(v)
    return (hp, hs) + tuple(jnp.stack(o) for o in outs)
```

```python
import functools
import math

import jax
import jax.numpy as jnp
from jax import lax
from jax.experimental import pallas as pl
from jax.experimental.pallas import tpu as pltpu

F32 = jnp.float32
BF16 = jnp.bfloat16
NEG = -1e30

HEAD_DIM = 128
NSA_HEADS = 8
NSA_GROUPS = 2
NSA_HPG = NSA_HEADS // NSA_GROUPS
CMP_LEN = 32
CMP_STRIDE = 16
SEL_BLOCK = 64
N_SEL = 16
WINDOW = 512
SB_HEADS = 4
MEM_HEADS = 4
CONV_W = 3
QB = 128
TPAD = 8
LN_EPS = 1e-5
PAGES_PER_STEP = 16
SB_CUT = 88.0
VMEM_LIMIT = 56 * 1024 * 1024

NSA_Q = NSA_HEADS * HEAD_DIM
KVG = NSA_GROUPS * HEAD_DIM
SB_W = SB_HEADS * HEAD_DIM
MEM_W = MEM_HEADS * HEAD_DIM
KV_CH = 4 * KVG + 2 * SB_W
WIN_CH = 2 * KVG
OFF_KV = NSA_Q
OFF_WIN = OFF_KV + KV_CH
OFF_QSB = OFF_WIN + WIN_CH
OFF_GNSA = OFF_QSB + SB_W + MEM_W
OFF_GMERGE = OFF_GNSA + 3 * NSA_HEADS
SCALE = HEAD_DIM ** -0.5


def _cparams(*sem):
    return pltpu.CompilerParams(dimension_semantics=sem, vmem_limit_bytes=VMEM_LIMIT)


def _nt(a, b):
    return lax.dot_general(a, b, (((1,), (1,)), ((), ())), preferred_element_type=F32)


def _dot(a, b):
    return jnp.dot(a, b, preferred_element_type=F32)


def _iota(shape, dim):
    return lax.broadcasted_iota(jnp.int32, shape, dim)


def _softmax_rows(s, mask):
    s = jnp.where(mask, s, NEG)
    m = jnp.max(s, axis=-1, keepdims=True)
    e = jnp.where(mask, jnp.exp(s - m), 0.0)
    return e / jnp.maximum(jnp.sum(e, axis=-1, keepdims=True), 1e-30)


def _split3(x):
    hi = x.astype(BF16)
    r1 = x - hi.astype(F32)
    mid = r1.astype(BF16)
    lo = (r1 - mid.astype(F32)).astype(BF16)
    return hi, mid, lo


def _softplus_pair(z):
    lg = jnp.log1p(jnp.exp(-jnp.abs(z)))
    return jnp.maximum(z, 0.0) + lg, jnp.minimum(z, 0.0) - lg


def _rank_select(score, lane, n_blk, n_top):
    cnt = jnp.zeros(score.shape, jnp.int32)
    for j in range(n_blk):
        col = score[:, j:j + 1]
        beats = (col > score) | ((col == score) & (lane > j))
        cnt = cnt + jnp.where(beats, 1, 0)
    return jnp.where((cnt < n_top) & (score >= 0.0), 1.0, 0.0)


def _rank_select_t(score_t, n_blk, n_top):
    blk = _iota((score_t.shape[0], 1), 0)
    cnt = jnp.zeros(score_t.shape, jnp.int32)
    for j in range(n_blk):
        row = score_t[j:j + 1, :]
        beats = (row > score_t) | ((row == score_t) & (blk > j))
        cnt = cnt + jnp.where(beats, 1, 0)
    return jnp.where((cnt < n_top) & (score_t >= 0.0), 1.0, 0.0)


def _pages_per_step(n_pages):
    return next(n for n in range(PAGES_PER_STEP, 0, -1) if n_pages % n == 0)


def _page_rows(new, cols):
    x = new[:, cols].astype(BF16)
    return jnp.concatenate([x, jnp.zeros((QB - x.shape[0], x.shape[1]), BF16)], axis=0)


def _block_scores(imp, pos, lane, n_blk):
    cur = pos // SEL_BLOCK
    forced = (lane == 0) | (lane == cur) | (lane == cur - 1)
    visible = lane * SEL_BLOCK <= pos
    score = jnp.where(forced, 1e9, jnp.where(visible, imp, -1.0))
    return jnp.where(lane < n_blk, score, -2.0)


def _overlap(nsub, lanes, n_cmp, n_blk, transposed=False):
    shape = (lanes, nsub) if transposed else (nsub, lanes)
    n = _iota(shape, 1 if transposed else 0)
    j = _iota(shape, 0 if transposed else 1)
    st = n * CMP_STRIDE
    ov = (st < (j + 1) * SEL_BLOCK) & (st + CMP_LEN > j * SEL_BLOCK) & (n < n_cmp) & (j < n_blk)
    return jnp.where(ov, 1.0, 0.0).astype(BF16)


def _mm_body(x_ref, w_ref, b_ref, *o_refs, act):
    r = _dot(x_ref[...], w_ref[...]) + b_ref[...]
    if act == "sigmoid":
        r = jax.nn.sigmoid(r)
    for o in o_refs:
        o[...] = r.astype(o.dtype)


def _matmul(x, w, b, out_dtypes, act=None, tm=1024, tn=512):
    M, K = x.shape
    N = w.shape[1]
    tm, tn = min(tm, M), min(tn, N)
    assert M % tm == 0 and N % tn == 0
    return pl.pallas_call(
        functools.partial(_mm_body, act=act),
        grid=(M // tm, N // tn),
        in_specs=[pl.BlockSpec((tm, K), lambda i, j: (i, 0)),
                  pl.BlockSpec((K, tn), lambda i, j: (0, j)),
                  pl.BlockSpec((1, tn), lambda i, j: (0, j))],
        out_specs=[pl.BlockSpec((tm, tn), lambda i, j: (i, j)) for _ in out_dtypes],
        out_shape=[jax.ShapeDtypeStruct((M, N), d) for d in out_dtypes],
        compiler_params=_cparams("parallel", "parallel"),
    )(x, w, b.reshape(1, N).astype(F32))


def _mm_ln_body(x_ref, w_ref, b_ref, res_ref, g_ref, be_ref, *rest, nk, alpha, n_out):
    o_refs, acc_ref = rest[:n_out], rest[n_out]
    k = pl.program_id(1)

    part = _dot(x_ref[...], w_ref[...])

    @pl.when(k == 0)
    def _():
        acc_ref[...] = part

    @pl.when(k > 0)
    def _():
        acc_ref[...] += part

    @pl.when(k == nk - 1)
    def _():
        v = alpha * res_ref[...] + (acc_ref[...] + b_ref[...])
        mu = jnp.mean(v, axis=-1, keepdims=True)
        d = v - mu
        var = jnp.mean(d * d, axis=-1, keepdims=True)
        y = d * lax.rsqrt(var + LN_EPS) * g_ref[...] + be_ref[...]
        for o in o_refs:
            o[...] = y.astype(o.dtype)


def _matmul_ln(x, w, b, res, g, be, alpha, out_dtypes, tm=512, tk_max=1536):
    M, K = x.shape
    N = w.shape[1]
    tm = min(tm, M)
    tk = max(t for t in range(128, min(tk_max, K) + 1, 128) if K % t == 0)
    assert M % tm == 0 and K % tk == 0
    nk = K // tk
    row = lambda a: a.reshape(1, N).astype(F32)
    return pl.pallas_call(
        functools.partial(_mm_ln_body, nk=nk, alpha=alpha, n_out=len(out_dtypes)),
        grid=(M // tm, nk),
        in_specs=[pl.BlockSpec((tm, tk), lambda i, k: (i, k)),
                  pl.BlockSpec((tk, N), lambda i, k: (k, 0)),
                  pl.BlockSpec((1, N), lambda i, k: (0, 0)),
                  pl.BlockSpec((tm, N), lambda i, k: (i, 0)),
                  pl.BlockSpec((1, N), lambda i, k: (0, 0)),
                  pl.BlockSpec((1, N), lambda i, k: (0, 0))],
        out_specs=[pl.BlockSpec((tm, N), lambda i, k: (i, 0)) for _ in out_dtypes],
        out_shape=[jax.ShapeDtypeStruct((M, N), d) for d in out_dtypes],
        scratch_shapes=[pltpu.VMEM((tm, N), F32)],
        compiler_params=_cparams("parallel", "arbitrary"),
    )(x, w, row(b), res, row(g), row(be))


def _merge_body(on_ref, os_ref, om_ref, g0_ref, g1_ref, g2_ref, wn_ref, ws_ref, wm_ref, o_ref):
    r = (g0_ref[...] * _dot(on_ref[...], wn_ref[...])
         + g1_ref[...] * _dot(os_ref[...], ws_ref[...])
         + g2_ref[...] * _dot(om_ref[...], wm_ref[...]))
    o_ref[...] = r.astype(o_ref.dtype)


def _merge(o_nsa, o_sb, o_mem, gm, w_nsa, w_sb, w_mem, tm=1024, tn=512):
    M = o_nsa.shape[0]
    D = w_nsa.shape[1]
    tm, tn = min(tm, M), min(tn, D)
    nj = D // tn
    xs = lambda a: pl.BlockSpec((tm, a.shape[1]), lambda i, j: (i, 0))
    ws = lambda a: pl.BlockSpec((a.shape[0], tn), lambda i, j: (0, j))
    gs = lambda c: pl.BlockSpec((tm, tn), lambda i, j: (i, c * nj + j))
    return pl.pallas_call(
        _merge_body,
        grid=(M // tm, nj),
        in_specs=[xs(o_nsa), xs(o_sb), xs(o_mem), gs(0), gs(1), gs(2), ws(w_nsa), ws(w_sb), ws(w_mem)],
        out_specs=pl.BlockSpec((tm, tn), lambda i, j: (i, j)),
        out_shape=jax.ShapeDtypeStruct((M, D), BF16),
        compiler_params=_cparams("parallel", "parallel"),
    )(o_nsa, o_sb, o_mem, gm, gm, gm, w_nsa, w_sb, w_mem)


def _conv_taps(u, p1, p2, wc_ref, bc_ref):
    return bc_ref[...] + wc_ref[0:1, :] * p2 + wc_ref[1:2, :] * p1 + wc_ref[2:3, :] * u


def _ffn_up_seq_body(x_ref, wa_ref, wg_ref, ba_ref, bg_ref, wca_ref, wcg_ref, bca_ref, bcg_ref, ha_ref, hg_ref,
                     act_ref, sa_ref, sg_ref, ca_ref, cg_ref, *, nt, tm):
    t = pl.program_id(2)

    @pl.when(t == 0)
    def _():
        ca_ref[0:2, :] = ha_ref[0]
        cg_ref[0:2, :] = hg_ref[0]

    x = x_ref[0]
    r = _iota((tm, 1), 0)

    def half(w_ref, b_ref, wc_ref, bc_ref, c_ref, s_ref):
        u = _dot(x, w_ref[...]) + b_ref[...]
        h0, h1 = c_ref[0:1, :], c_ref[1:2, :]
        p1 = jnp.where(r == 0, h1, pltpu.roll(u, 1, 0))
        p2 = jnp.where(r == 0, h0, jnp.where(r == 1, h1, pltpu.roll(u, 2, 0)))
        c = _conv_taps(u, p1, p2, wc_ref, bc_ref)
        c_ref[0:2, :] = u[tm - 2:tm, :]

        @pl.when(t == nt - 1)
        def _():
            s_ref[0] = u[tm - 2:tm, :]

        return c

    a = half(wa_ref, ba_ref, wca_ref, bca_ref, ca_ref, sa_ref)
    g = half(wg_ref, bg_ref, wcg_ref, bcg_ref, cg_ref, sg_ref)
    act_ref[0] = (a * jax.nn.gelu(g, approximate=True)).astype(act_ref.dtype)


def _ffn_up_seq(x, w_a, w_g, b_a, b_g, wc_a, wc_g, bc_a, bc_g, h_a, h_g, tm=1024, tn=512):
    B, T, D = x.shape
    Fh = w_a.shape[1]
    tm, tn = min(tm, T), min(tn, Fh)
    assert T % tm == 0 and Fh % tn == 0 and tm >= 8
    nt = T // tm
    wsp = pl.BlockSpec((D, tn), lambda j, b, t: (0, j))
    rsp = pl.BlockSpec((1, tn), lambda j, b, t: (0, j))
    csp = pl.BlockSpec((CONV_W, tn), lambda j, b, t: (0, j))
    hsp = pl.BlockSpec((1, 2, tn), lambda j, b, t: (b, 0, j))
    row = lambda a: a.reshape(1, Fh)
    return pl.pallas_call(
        functools.partial(_ffn_up_seq_body, nt=nt, tm=tm),
        grid=(Fh // tn, B, nt),
        in_specs=[pl.BlockSpec((1, tm, D), lambda j, b, t: (b, t, 0)), wsp, wsp, rsp, rsp, csp, csp, rsp, rsp, hsp, hsp],
        out_specs=[pl.BlockSpec((1, tm, tn), lambda j, b, t: (b, t, j)), hsp, hsp],
        out_shape=[jax.ShapeDtypeStruct((B, T, Fh), BF16),
                   jax.ShapeDtypeStruct((B, 2, Fh), F32), jax.ShapeDtypeStruct((B, 2, Fh), F32)],
        scratch_shapes=[pltpu.VMEM((8, tn), F32), pltpu.VMEM((8, tn), F32)],
        compiler_params=_cparams("parallel", "parallel", "arbitrary"),
    )(x, w_a, w_g, row(b_a), row(b_g), wc_a, wc_g, row(bc_a), row(bc_g), h_a, h_g)


def _ffn_up_rows_body(x_ref, wa_ref, wg_ref, ba_ref, bg_ref, wca_ref, wcg_ref, bca_ref, bcg_ref,
                      h1a_ref, h2a_ref, h1g_ref, h2g_ref, act_ref, ua_ref, ug_ref, *, period):
    x = x_ref[...]
    tin = _iota((x.shape[0], 1), 0) % period

    def half(w_ref, b_ref, wc_ref, bc_ref, h1_ref, h2_ref, u_ref):
        u = _dot(x, w_ref[...]) + b_ref[...]
        u_ref[...] = u
        p1 = jnp.where(tin >= 1, pltpu.roll(u, 1, 0), h1_ref[...])
        p2 = jnp.where(tin >= 2, pltpu.roll(u, 2, 0), h2_ref[...])
        return _conv_taps(u, p1, p2, wc_ref, bc_ref)

    a = half(wa_ref, ba_ref, wca_ref, bca_ref, h1a_ref, h2a_ref, ua_ref)
    g = half(wg_ref, bg_ref, wcg_ref, bcg_ref, h1g_ref, h2g_ref, ug_ref)
    act_ref[...] = (a * jax.nn.gelu(g, approximate=True)).astype(act_ref.dtype)


def _ffn_up_rows(x, w_a, w_g, b_a, b_g, wc_a, wc_g, bc_a, bc_g, h1a, h2a, h1g, h2g, period, tn=512):
    M, D = x.shape
    Fh = w_a.shape[1]
    tn = min(tn, Fh)
    wsp = pl.BlockSpec((D, tn), lambda j: (0, j))
    rsp = pl.BlockSpec((1, tn), lambda j: (0, j))
    csp = pl.BlockSpec((CONV_W, tn), lambda j: (0, j))
    msp = pl.BlockSpec((M, tn), lambda j: (0, j))
    row = lambda a: a.reshape(1, Fh)
    return pl.pallas_call(
        functools.partial(_ffn_up_rows_body, period=period),
        grid=(Fh // tn,),
        in_specs=[pl.BlockSpec((M, D), lambda j: (0, 0)), wsp, wsp, rsp, rsp, csp, csp, rsp, rsp, msp, msp, msp, msp],
        out_specs=[msp, msp, msp],
        out_shape=[jax.ShapeDtypeStruct((M, Fh), BF16), jax.ShapeDtypeStruct((M, Fh), F32),
                   jax.ShapeDtypeStruct((M, Fh), F32)],
        compiler_params=_cparams("parallel"),
    )(x, w_a, w_g, row(b_a), row(b_g), wc_a, wc_g, row(bc_a), row(bc_g), h1a, h2a, h1g, h2g)


def _compress_body(pt_ref, *refs, n_steps, nsub, npg):
    page_refs = refs[:npg]
    (w1k_ref, w1v_ref, pek_ref, pev_ref, wpk_ref, wpv_ref, perm_ref, w2k_ref, w2v_ref,
     ok_ref, ov_ref, xs_ref) = refs[npg:]
    st = pl.program_id(1)
    sub = QB // CMP_STRIDE
    for jp in range(npg // 2):
        x2 = jnp.concatenate([page_refs[2 * jp][0], page_refs[2 * jp + 1][0]], axis=0).astype(BF16)
        xp = _dot(perm_ref[...], x2).astype(BF16)
        row0 = pl.multiple_of((st * npg + 2 * jp) * sub, 2 * sub)
        for s in range(CMP_STRIDE):
            xs_ref[s, pl.ds(row0, 2 * sub), :] = xp[s * 2 * sub:(s + 1) * 2 * sub, :]

    @pl.when(st == n_steps - 1)
    def _():
        for kv, (w1_ref, pe_ref, wp_ref, w2_ref, o_ref) in enumerate(
                ((w1k_ref, pek_ref, wpk_ref, w2k_ref, ok_ref), (w1v_ref, pev_ref, wpv_ref, w2v_ref, ov_ref))):
            acc = jnp.zeros((NSA_GROUPS * nsub, 2 * HEAD_DIM), F32)
            for s in range(0, CMP_STRIDE, 2):
                xg = jnp.concatenate(
                    [jnp.concatenate([xs_ref[s + d, :, kv * KVG + g * HEAD_DIM:kv * KVG + (g + 1) * HEAD_DIM]
                                      for d in range(2)], axis=1) for g in range(NSA_GROUPS)], axis=0)
                acc = acc + _dot(xg, w1_ref[s // 2])
            pe_term = _dot(pe_ref[...], wp_ref[...])[0:1, :]
            hid = acc[:, :HEAD_DIM] + pltpu.roll(acc[:, HEAD_DIM:], NSA_GROUPS * nsub - 1, 0) + pe_term
            out = _dot(jax.nn.gelu(hid, approximate=True).astype(BF16), w2_ref[...])
            for g in range(NSA_GROUPS):
                o_ref[0, g] = out[g * nsub:(g + 1) * nsub, :].astype(o_ref.dtype)


def _compress(pages, table, w1k, w1v, pek, pev, w2k, w2v):
    B, P = table.shape
    nsub = P * (QB // CMP_STRIDE)
    assert CMP_LEN == 2 * CMP_STRIDE
    npg = _pages_per_step(P)
    assert npg % 2 == 0
    sub = QB // CMP_STRIDE

    o = jnp.arange(2 * QB)
    src = ((o % (2 * sub)) // sub) * QB + CMP_STRIDE * (o % sub) + o // (2 * sub)
    perm = (src[:, None] == o[None, :]).astype(BF16)

    def pair_weights(w1):
        w = w1.reshape(2, CMP_STRIDE // 2, 2, HEAD_DIM, HEAD_DIM).transpose(1, 2, 3, 0, 4)
        return w.reshape(CMP_STRIDE // 2, 2 * HEAD_DIM, 2 * HEAD_DIM)

    flat_pe = lambda pe: jnp.broadcast_to(pe.reshape(1, CMP_LEN * HEAD_DIM), (8, CMP_LEN * HEAD_DIM)).astype(BF16)
    flat_w = lambda w1: w1.reshape(CMP_LEN * HEAD_DIM, HEAD_DIM)

    full = lambda a: pl.BlockSpec(a.shape, lambda b, s, pt: (0,) * a.ndim)
    osp = pl.BlockSpec((1, NSA_GROUPS, nsub, HEAD_DIM), lambda b, s, pt: (b, 0, 0, 0))
    osh = jax.ShapeDtypeStruct((B, NSA_GROUPS, nsub, HEAD_DIM), BF16)
    page = lambda j: pl.BlockSpec((1, QB, 2 * KVG), lambda b, s, pt: (pt[b, s * npg + j], 0, 0))
    consts = (pair_weights(w1k), pair_weights(w1v), flat_pe(pek), flat_pe(pev), flat_w(w1k), flat_w(w1v), perm,
              w2k, w2v)
    return pl.pallas_call(
        functools.partial(_compress_body, n_steps=P // npg, nsub=nsub, npg=npg),
        grid_spec=pltpu.PrefetchScalarGridSpec(
            num_scalar_prefetch=1, grid=(B, P // npg),
            in_specs=[page(j) for j in range(npg)] + [full(a) for a in consts],
            out_specs=[osp, osp],
            scratch_shapes=[pltpu.VMEM((CMP_STRIDE, nsub, 2 * KVG), BF16)]),
        out_shape=[osh, osh],
        compiler_params=_cparams("parallel", "arbitrary"),
    )(table, *([pages] * npg), *consts)


def _nsa_seq_body(q_ref, kc_ref, vc_ref, ks_ref, vs_ref, kw_ref, vw_ref, gate_ref, slope_ref, o_ref,
                  s_ref, mx_ref, sum_ref, acc_ref, *, n_cmp, n_blk, n_top, unroll_slc, unroll_win):
    g = pl.program_id(1)
    i = pl.program_id(2)
    t0 = i * QB
    R = NSA_HPG * QB
    q = q_ref[0]
    qs = jnp.concatenate([q[:, h * HEAD_DIM:(h + 1) * HEAD_DIM] for h in range(NSA_HPG)], axis=0)
    rowt = _iota((R, 1), 0) % QB
    pos = t0 + rowt
    slope = slope_ref[0]
    lane = _iota((1, QB), 1)

    kc = kc_ref[0, 0]
    nsub = kc.shape[0]
    ncol = _iota((1, nsub), 1)
    dist_c = pos - (ncol * CMP_STRIDE + (CMP_LEN - 1))
    s = _nt(qs, kc) * SCALE - slope * dist_c.astype(F32)
    p_c = _softmax_rows(s, (dist_c >= 0) & (ncol < n_cmp)).astype(BF16)
    o_cmp = _dot(p_c, vc_ref[0, 0])
    imp4 = _nt(_overlap(nsub, QB, n_cmp, n_blk, transposed=True), p_c)
    imp_t = imp4[:, 0:QB]
    for h in range(1, NSA_HPG):
        imp_t = imp_t + imp4[:, h * QB:(h + 1) * QB]
    blk = _iota((QB, 1), 0)
    posq = t0 + lane
    cur = posq // SEL_BLOCK
    forced = (blk == 0) | (blk == cur) | (blk == cur - 1)
    score_t = jnp.where(forced, 1e9, jnp.where(blk * SEL_BLOCK <= posq, imp_t, -1.0))
    score_t = jnp.where(blk < n_blk, score_t, -2.0)
    nb = 8 * (-(-n_blk // 8))
    sel_t = _rank_select_t(score_t[0:nb], n_blk, n_top)
    if nb < QB:
        sel_t = jnp.concatenate([sel_t, jnp.zeros((QB - nb, QB), F32)], axis=0)
    sel = sel_t.T.astype(BF16)

    slope_b = jnp.broadcast_to(slope, (R, QB))
    rk = (rowt - lane).astype(F32)

    def attend(lo, hi, k_ref, v_ref, mask_fn, unroll):
        mx_ref[...] = jnp.full(mx_ref.shape, NEG, F32)
        g_lo, g_hi = lo // unroll, (hi + unroll - 1) // unroll

        def scores(cg, carry):
            for u in range(unroll):
                c = cg * unroll + u
                k0 = pl.multiple_of(c * QB, QB)
                dist = rk + (t0 - k0).astype(F32)
                sc = _nt(qs, k_ref[0, pl.ds(k0, QB), :]) * SCALE - slope_b * dist
                sc = jnp.where(mask_fn(c, dist), sc, NEG)
                s_ref[:, pl.ds(k0, QB)] = sc
                mx_ref[...] = jnp.maximum(mx_ref[...], sc)
            return carry

        lax.fori_loop(g_lo, g_hi, scores, 0)
        m = jnp.max(mx_ref[...], axis=-1, keepdims=True)
        mx_ref[...] = jnp.broadcast_to(m, (R, QB))
        sum_ref[...] = jnp.zeros_like(sum_ref)
        acc_ref[...] = jnp.zeros_like(acc_ref)

        def weigh(cg, carry):
            for u in range(unroll):
                k0 = pl.multiple_of((cg * unroll + u) * QB, QB)
                p = jnp.exp(s_ref[:, pl.ds(k0, QB)] - mx_ref[...])
                sum_ref[...] += p
                acc_ref[...] += _dot(p.astype(BF16), v_ref[0, pl.ds(k0, QB), :])
            return carry

        lax.fori_loop(g_lo, g_hi, weigh, 0)
        l = jnp.sum(sum_ref[...], axis=-1, keepdims=True)
        return acc_ref[...] / jnp.maximum(l, 1e-30)

    def slc_mask(c, dist):
        j = _iota((QB, QB), 0)
        k = _iota((QB, QB), 1)
        e = jnp.where(j == 2 * c + k // SEL_BLOCK, 1.0, 0.0).astype(BF16)
        mk = _dot(sel, e)
        mk = jnp.concatenate([mk] * NSA_HPG, axis=0)
        return (mk > 0.5) & (dist >= 0.0)

    o_slc = attend(0, i + 1, ks_ref, vs_ref, slc_mask, unroll_slc)
    o_win = attend(jnp.maximum(i - WINDOW // QB, 0), i + 1, kw_ref, vw_ref,
                   lambda c, dist: (dist >= 0.0) & (dist < float(WINDOW)), unroll_win)

    gate = gate_ref[0]
    outs = []
    for h in range(NSA_HPG):
        rows = slice(h * QB, (h + 1) * QB)
        o_h = jnp.zeros((QB, HEAD_DIM), F32)
        for br, o_br in enumerate((o_cmp, o_slc, o_win)):
            col = br * NSA_HEADS + g * NSA_HPG + h
            gcol = jnp.sum(jnp.where(lane == col, gate, 0.0), axis=-1, keepdims=True)
            o_h = o_h + gcol * o_br[rows]
        outs.append(o_h)
    o_ref[0] = jnp.concatenate(outs, axis=-1).astype(o_ref.dtype)


def _nsa_seq(q, kcmp, vcmp, kvb, winb, gates, slopes):
    B, T, _ = q.shape
    nq = T // QB
    nsub = kcmp.shape[2]
    n_blk = -(-T // SEL_BLOCK)
    assert T % QB == 0 and n_blk <= QB and WINDOW % QB == 0
    divisor = lambda n: next(u for u in range(n, 0, -1) if nq % u == 0)
    R = NSA_HPG * QB
    csp = pl.BlockSpec((1, 1, nsub, HEAD_DIM), lambda b, g, i: (b, g, 0, 0))
    col = lambda c0: pl.BlockSpec((1, T, HEAD_DIM), lambda b, g, i: (b, 0, c0 + g))
    tile = pltpu.VMEM((R, QB), F32)
    return pl.pallas_call(
        functools.partial(_nsa_seq_body, n_cmp=nsub - 1, n_blk=n_blk, n_top=min(N_SEL, n_blk),
                          unroll_slc=divisor(4), unroll_win=divisor(2)),
        grid=(B, NSA_GROUPS, nq),
        in_specs=[pl.BlockSpec((1, QB, NSA_HPG * HEAD_DIM), lambda b, g, i: (b, i, g)), csp, csp,
                  col(2 * NSA_GROUPS), col(3 * NSA_GROUPS), col(0), col(NSA_GROUPS),
                  pl.BlockSpec((1, QB, 128), lambda b, g, i: (b, i, 0)),
                  pl.BlockSpec((1, R, 1), lambda b, g, i: (g, 0, 0))],
        out_specs=pl.BlockSpec((1, QB, NSA_HPG * HEAD_DIM), lambda b, g, i: (b, i, g)),
        out_shape=jax.ShapeDtypeStruct((B, T, NSA_Q), BF16),
        scratch_shapes=[pltpu.VMEM((R, T), F32), tile, tile, tile],
        compiler_params=_cparams("parallel", "parallel", "arbitrary"),
    )(q, kcmp, vcmp, kvb, kvb, winb, winb, gates, slopes)


def _sb_chunk(z, before, r, upper):
    sp, ls = _softplus_pair(z)
    nlk = jnp.where(before, sp, 0.0)
    hi, mid, lo = _split3(nlk)
    later = _dot(hi, upper) + _dot(mid, upper) + _dot(lo, upper)
    a = jnp.where(before, jnp.exp(ls - (r + later)), 0.0)
    return a, jnp.sum(nlk, axis=-1, keepdims=True)


def _upper(n):
    return jnp.where(_iota((n, n), 0) > _iota((n, n), 1), 1.0, 0.0).astype(BF16)


def _sbmem_seq_body(qs_ref, qm_ref, k_ref, v_ref, km_ref, vm_ref, osb_ref, om_ref, r_ref, acc_ref):
    i = pl.program_id(1)
    t0 = i * QB
    pos = t0 + _iota((QB, 1), 0)
    lane = _iota((1, QB), 1)
    upper = _upper(QB)
    r_ref[...] = jnp.zeros_like(r_ref)
    acc_ref[...] = jnp.zeros_like(acc_ref)

    def body(carry):
        step, _ = carry
        k0 = pl.multiple_of((i - step) * QB, QB)
        before = (pos - (k0 + lane)) > 0
        heads = [slice(h * HEAD_DIM, (h + 1) * HEAD_DIM) for h in range(SB_HEADS)]
        pairs = [_softplus_pair(_nt(qs_ref[0, :, c], k_ref[0, pl.ds(k0, QB), c]) * SCALE) for c in heads]
        nlk = [jnp.where(before, sp, 0.0) for sp, _ in pairs]
        later = _dot(jnp.concatenate([t for x in nlk for t in _split3(x)], axis=0), upper)
        r_new = []
        for h, c in enumerate(heads):
            rows = later[3 * h * QB:(3 * h + 3) * QB]
            between = r_ref[h] + rows[0:QB] + rows[QB:2 * QB] + rows[2 * QB:3 * QB]
            a = jnp.where(before, jnp.exp(pairs[h][1] - between), 0.0)
            acc_ref[h] += _dot(a.astype(BF16), v_ref[0, pl.ds(k0, QB), c])
            r_new.append(r_ref[h] + jnp.sum(nlk[h], axis=-1, keepdims=True))
            r_ref[h] = r_new[h]
        rmin = jnp.min(functools.reduce(jnp.minimum, r_new))
        return step + 1, rmin

    lax.while_loop(lambda c: (c[0] <= i) & (c[1] < SB_CUT), body, (jnp.int32(0), jnp.float32(0.0)))
    for h in range(SB_HEADS):
        cols = slice(h * HEAD_DIM, (h + 1) * HEAD_DIM)
        osb_ref[0, :, cols] = acc_ref[h].astype(osb_ref.dtype)
        s = _nt(qm_ref[0, :, cols], km_ref[0, :, cols]) * SCALE
        p = _softmax_rows(s, jnp.full(s.shape, True))
        om_ref[0, :, cols] = _dot(p.astype(BF16), vm_ref[0, :, cols]).astype(om_ref.dtype)


def _sbmem_seq(qsm, kvb, memb):
    B, T, _ = qsm.shape
    Mt = memb.shape[1]
    nq = T // QB
    assert SB_W == MEM_W and (4 * KVG) % SB_W == 0
    qsp = lambda c: pl.BlockSpec((1, QB, SB_W), lambda b, i: (b, i, c))
    ksp = lambda c: pl.BlockSpec((1, T, SB_W), lambda b, i: (b, 0, c))
    msp = lambda c: pl.BlockSpec((1, Mt, MEM_W), lambda b, i: (b, 0, c))
    osp = pl.BlockSpec((1, QB, SB_W), lambda b, i: (b, i, 0))
    sb0 = 4 * KVG // SB_W
    return pl.pallas_call(
        _sbmem_seq_body,
        grid=(B, nq),
        in_specs=[qsp(0), qsp(1), ksp(sb0), ksp(sb0 + 1), msp(0), msp(1)],
        out_specs=[osp, osp],
        out_shape=[jax.ShapeDtypeStruct((B, T, SB_W), BF16), jax.ShapeDtypeStruct((B, T, MEM_W), BF16)],
        scratch_shapes=[pltpu.VMEM((SB_HEADS, QB, 1), F32), pltpu.VMEM((SB_HEADS, QB, HEAD_DIM), F32)],
        compiler_params=_cparams("parallel", "arbitrary"),
    )(qsm, qsm, kvb, kvb, memb, memb)


def _dec_local_body(q_ref, kc_ref, vc_ref, win_ref, wnew_ref, qm_ref, mem_ref, slope_ref, ocmp_ref, owin_ref, omem_ref,
                    sel_ref, *, n_cmp, n_blk, n_top, pos0, n_new):
    R = NSA_HPG * TPAD
    row = _iota((R, 1), 0)
    pos = pos0 + row % TPAD
    nsub = kc_ref.shape[2]
    lanes = sel_ref.shape[3]
    lane = _iota((1, lanes), 1)
    ncol = _iota((1, nsub), 1)
    n_hist = win_ref.shape[1]
    kidx = _iota((1, n_hist + QB), 1)
    kp = pos0 - n_hist + kidx
    ov = _overlap(nsub, lanes, n_cmp, n_blk)
    for g in range(NSA_GROUPS):
        q = q_ref[0, g]
        slope = slope_ref[g]
        dist_c = pos - (ncol * CMP_STRIDE + (CMP_LEN - 1))
        s = _nt(q, kc_ref[0, g]) * SCALE - slope * dist_c.astype(F32)
        p_c = _softmax_rows(s, (dist_c >= 0) & (ncol < n_cmp)).astype(BF16)
        ocmp_ref[0, g] = _dot(p_c, vc_ref[0, g])
        imp4 = _dot(p_c, ov)
        imp = imp4[0:TPAD]
        for h in range(1, NSA_HPG):
            imp = imp + imp4[h * TPAD:(h + 1) * TPAD]
        score = _block_scores(imp, pos0 + _iota((TPAD, 1), 0), lane, n_blk)
        sel_ref[0, g] = _rank_select(score, lane, n_blk, n_top)

        kcols = slice(g * HEAD_DIM, (g + 1) * HEAD_DIM)
        vcols = slice(KVG + g * HEAD_DIM, KVG + (g + 1) * HEAD_DIM)
        kw = jnp.concatenate([win_ref[0, :, kcols].astype(BF16), _page_rows(wnew_ref[0], kcols)], axis=0)
        vw = jnp.concatenate([win_ref[0, :, vcols].astype(BF16), _page_rows(wnew_ref[0], vcols)], axis=0)
        dist_w = pos - kp
        s = _nt(q, kw) * SCALE - slope * dist_w.astype(F32)
        ok = (kidx < n_hist + n_new) & (kp >= 0) & (dist_w >= 0) & (dist_w < WINDOW)
        owin_ref[0, g] = _dot(_softmax_rows(s, ok).astype(BF16), vw)

    km = mem_ref[0, :, :MEM_W].astype(BF16)
    vm = mem_ref[0, :, MEM_W:].astype(BF16)
    s = _nt(qm_ref[0], km) * SCALE
    o_all = _dot(_softmax_rows(s, jnp.full(s.shape, True)).astype(BF16), vm)
    hrow = _iota((MEM_HEADS * TPAD, 1), 0) // TPAD
    o = jnp.zeros((MEM_HEADS * TPAD, HEAD_DIM), F32)
    for h in range(MEM_HEADS):
        o = o + jnp.where(hrow == h, o_all[:, h * HEAD_DIM:(h + 1) * HEAD_DIM], 0.0)
    omem_ref[0] = o


def _dec_local(q, kcmp, vcmp, win_buf, win_new, qm_bd, mem, slopes, pos0, n_new, n_blk):
    B = q.shape[0]
    R = NSA_HPG * TPAD
    nsub = kcmp.shape[2]
    lanes = 128 * (-(-n_blk // 128))
    b4 = lambda a: pl.BlockSpec((1,) + a.shape[1:], lambda b: (b,) + (0,) * (a.ndim - 1))
    osh = jax.ShapeDtypeStruct((B, NSA_GROUPS, R, HEAD_DIM), F32)
    osp = pl.BlockSpec((1, NSA_GROUPS, R, HEAD_DIM), lambda b: (b, 0, 0, 0))
    return pl.pallas_call(
        functools.partial(_dec_local_body, n_cmp=nsub - 1, n_blk=n_blk, n_top=min(N_SEL, n_blk), pos0=pos0,
                          n_new=n_new),
        grid=(B,),
        in_specs=[b4(q), b4(kcmp), b4(vcmp), b4(win_buf), b4(win_new), b4(qm_bd), b4(mem),
                  pl.BlockSpec(slopes.shape, lambda b: (0, 0, 0))],
        out_specs=[osp, osp, pl.BlockSpec((1, R, HEAD_DIM), lambda b: (b, 0, 0)),
                   pl.BlockSpec((1, NSA_GROUPS, TPAD, lanes), lambda b: (b, 0, 0, 0))],
        out_shape=[osh, osh, jax.ShapeDtypeStruct((B, R, HEAD_DIM), F32),
                   jax.ShapeDtypeStruct((B, NSA_GROUPS, TPAD, lanes), F32)],
        compiler_params=_cparams("parallel"),
    )(q, kcmp, vcmp, win_buf, win_new, qm_bd, mem, slopes)


def _dec_slc_body(pt_ref, *refs, npg, n_steps, pos0):
    page_refs = refs[:npg]
    new_ref, q_ref, sel_ref, seln_ref, e_ref, slope_ref, o_ref, m_ref, l_ref, acc_ref = refs[npg:]
    st = pl.program_id(1)
    R = NSA_HEADS * TPAD
    pos = pos0 + _iota((R, 1), 0) % TPAD
    slope = slope_ref[...]
    q = q_ref[0]
    slc0 = 2 * KVG

    def update(k, v, k0, chosen):
        dist = pos - (k0 + _iota((1, k.shape[0]), 1))
        mask = chosen & (dist >= 0)
        sc = jnp.where(mask, _nt(q, k) * SCALE - slope * dist.astype(F32), NEG)
        m_old = m_ref[...]
        m_new = jnp.maximum(m_old, jnp.max(sc, axis=-1, keepdims=True))
        a = jnp.exp(m_old - m_new)
        p = jnp.where(mask, jnp.exp(sc - m_new), 0.0)
        l_ref[...] = a * l_ref[...] + jnp.sum(p, axis=-1, keepdims=True)
        acc_ref[...] = a * acc_ref[...] + _dot(p.astype(BF16), v)
        m_ref[...] = m_new

    @pl.when(st == 0)
    def _():
        m_ref[...] = jnp.full(m_ref.shape, NEG, F32)
        l_ref[...] = jnp.zeros_like(l_ref)
        acc_ref[...] = jnp.zeros_like(acc_ref)
        new = new_ref[0]
        update(_page_rows(new, slice(slc0, slc0 + KVG)), _page_rows(new, slice(slc0 + KVG, slc0 + 2 * KVG)), pos0,
               seln_ref[0, :, 0:1] > 0.5)

    k = jnp.concatenate([p[0, :, 0:KVG].astype(BF16) for p in page_refs], axis=0)
    v = jnp.concatenate([p[0, :, KVG:2 * KVG].astype(BF16) for p in page_refs], axis=0)
    update(k, v, st * (npg * QB), _dot(sel_ref[0, 0], e_ref[...]) > 0.5)

    @pl.when(st == n_steps - 1)
    def _():
        o = acc_ref[...] / jnp.maximum(l_ref[...], 1e-30)
        rows = NSA_HPG * TPAD
        for g in range(NSA_GROUPS):
            o_ref[0, g] = o[g * rows:(g + 1) * rows, g * HEAD_DIM:(g + 1) * HEAD_DIM]


def _dec_slc(pages, table, new_page, q_rows, sel, slopes, pos0):
    B, P = table.shape
    rows = NSA_HPG * TPAD
    R = NSA_GROUPS * rows
    npg = _pages_per_step(P)
    n_steps = P // npg
    bps = npg * (QB // SEL_BLOCK)
    assert bps <= 128
    eye_g = jnp.eye(NSA_GROUPS, dtype=BF16)
    q_bd = (q_rows[:, :, :, None, :] * eye_g[None, :, None, :, None]).reshape(B, R, KVG)
    sel_rows = jnp.broadcast_to(sel[:, :, None], (B, NSA_GROUPS, NSA_HPG) + sel.shape[2:]).reshape(B, R, -1)
    lanes = lambda a: jnp.pad(a, [(0, 0)] * (a.ndim - 1) + [(0, 128 - a.shape[-1])]).astype(BF16)
    sel_steps = lanes(sel_rows[:, :, :n_steps * bps].reshape(B, R, n_steps, bps).transpose(0, 2, 1, 3))
    n_blk = sel.shape[-1]
    sel_new = lanes(sel_rows[:, :, n_steps * bps:n_steps * bps + 1]) if n_blk > n_steps * bps \
        else jnp.zeros((B, R, 128), BF16)
    expand = (_iota((128, npg * QB), 0) == _iota((128, npg * QB), 1) // SEL_BLOCK).astype(BF16)
    slope_rows = jnp.repeat(slopes.reshape(-1), TPAD)[:, None]

    b4 = lambda a: pl.BlockSpec((1,) + a.shape[1:], lambda b, s, pt: (b,) + (0,) * (a.ndim - 1))
    full = lambda a: pl.BlockSpec(a.shape, lambda b, s, pt: (0,) * a.ndim)
    page = lambda j: pl.BlockSpec((1, QB, 2 * KVG), lambda b, s, pt: (pt[b, s * npg + j], 0, 1))
    return pl.pallas_call(
        functools.partial(_dec_slc_body, npg=npg, n_steps=n_steps, pos0=pos0),
        grid_spec=pltpu.PrefetchScalarGridSpec(
            num_scalar_prefetch=1, grid=(B, n_steps),
            in_specs=[page(j) for j in range(npg)]
            + [b4(new_page), b4(q_bd), pl.BlockSpec((1, 1, R, 128), lambda b, s, pt: (b, s, 0, 0)), b4(sel_new),
               full(expand), full(slope_rows)],
            out_specs=pl.BlockSpec((1, NSA_GROUPS, rows, HEAD_DIM), lambda b, s, pt: (b, 0, 0, 0)),
            scratch_shapes=[pltpu.VMEM((R, 1), F32), pltpu.VMEM((R, 1), F32), pltpu.VMEM((R, KVG), F32)]),
        out_shape=jax.ShapeDtypeStruct((B, NSA_GROUPS, rows, HEAD_DIM), F32),
        compiler_params=_cparams("parallel", "arbitrary"),
    )(table, *([pages] * npg), new_page, q_bd, sel_steps, sel_new, expand, slope_rows)


def _dec_sb_body(pt_ref, pages_ref, new_ref, q_ref, o_ref, buf_ref, sem_ref, r_ref, acc_ref, *, n_pages, pos0, n_new):
    b = pl.program_id(0)
    R = SB_HEADS * TPAD
    row = _iota((R, 1), 0)
    pos = pos0 + row % TPAD
    live = row % TPAD < n_new
    lane = _iota((1, QB), 1)
    upper = _upper(QB)
    sb0 = 4 * KVG

    def page_copy(page, slot):
        return pltpu.make_async_copy(pages_ref.at[pt_ref[b, page], :, pl.ds(sb0, 2 * SB_W)], buf_ref.at[slot],
                                     sem_ref.at[slot])

    page_copy(n_pages - 1, 0).start()
    r_ref[...] = jnp.zeros_like(r_ref)
    acc_ref[...] = jnp.zeros_like(acc_ref)

    def chunk(k, v, k0):
        before = (pos - (k0 + lane)) > 0
        z = _nt(q_ref[0], k.astype(BF16)) * SCALE
        a, mass = _sb_chunk(z, before, r_ref[...], upper)
        acc_ref[...] += _dot(a.astype(BF16), v.astype(BF16))
        r_new = r_ref[...] + mass
        r_ref[...] = r_new
        return jnp.min(jnp.where(live, r_new, SB_CUT))

    new = new_ref[0]
    rmin0 = chunk(_page_rows(new, slice(sb0, sb0 + SB_W)), _page_rows(new, slice(sb0 + SB_W, sb0 + 2 * SB_W)), pos0)

    def body(carry):
        j, _ = carry
        slot = j % 2
        page = n_pages - 1 - j
        page_copy(page, slot).wait()

        @pl.when(j + 1 < n_pages)
        def _():
            page_copy(page - 1, 1 - slot).start()

        rmin = chunk(buf_ref[slot, :, 0:SB_W], buf_ref[slot, :, SB_W:2 * SB_W], page * QB)
        return j + 1, rmin

    done, _ = lax.while_loop(lambda c: (c[0] < n_pages) & (c[1] < SB_CUT), body, (jnp.int32(0), rmin0))

    @pl.when(done < n_pages)
    def _():
        page_copy(n_pages - 1 - done, done % 2).wait()

    hrow = row // TPAD
    o = jnp.zeros((R, HEAD_DIM), F32)
    for h in range(SB_HEADS):
        o = o + jnp.where(hrow == h, acc_ref[:, h * HEAD_DIM:(h + 1) * HEAD_DIM], 0.0)
    o_ref[0] = o


def _dec_sb(pages, table, new_page, q_bd, pos0, n_new):
    B, P = table.shape
    R = SB_HEADS * TPAD
    return pl.pallas_call(
        functools.partial(_dec_sb_body, n_pages=P, pos0=pos0, n_new=n_new),
        grid_spec=pltpu.PrefetchScalarGridSpec(
            num_scalar_prefetch=1, grid=(B,),
            in_specs=[pl.BlockSpec(memory_space=pl.ANY),
                      pl.BlockSpec((1, TPAD, KV_CH), lambda b, pt: (b, 0, 0)),
                      pl.BlockSpec((1, R, SB_W), lambda b, pt: (b, 0, 0))],
            out_specs=pl.BlockSpec((1, R, HEAD_DIM), lambda b, pt: (b, 0, 0)),
            scratch_shapes=[pltpu.VMEM((2, QB, 2 * SB_W), F32), pltpu.SemaphoreType.DMA((2,)),
                            pltpu.VMEM((R, 1), F32), pltpu.VMEM((R, SB_W), F32)]),
        out_shape=jax.ShapeDtypeStruct((B, R, HEAD_DIM), F32),
        compiler_params=_cparams("arbitrary"),
    )(table, pages, new_page, q_bd)


def _gate3_body(ga_ref, gb_ref, gc_ref, a_ref, b_ref, c_ref, o_ref):
    o_ref[...] = (ga_ref[...] * a_ref[...] + gb_ref[...] * b_ref[...] + gc_ref[...] * c_ref[...]).astype(o_ref.dtype)


def _gate3(gates, a, b, c):
    return pl.pallas_call(_gate3_body, out_shape=jax.ShapeDtypeStruct(a.shape, BF16))(*gates, a, b, c)


def _alibi_slopes(n):
    return jnp.exp2(-8.0 * jnp.arange(1, n + 1, dtype=F32) / n)


def _layer_weights(p):
    bf = lambda a: a.astype(BF16)
    w_in, b_in = p["w_in"], p["b_in"]
    seg = lambda a, b: (bf(w_in[:, a:b]), b_in[a:b])
    ngate = 3 * NSA_HEADS
    d_ff = p["w_down"].shape[0]
    pe_rows = lambda pe: pe.astype(F32)
    return dict(
        q=seg(0, OFF_KV), kv=seg(OFF_KV, OFF_WIN), win=seg(OFF_WIN, OFF_QSB), qsm=seg(OFF_QSB, OFF_GNSA),
        gn=(bf(jnp.pad(w_in[:, OFF_GNSA:OFF_GMERGE], ((0, 0), (0, 128 - ngate)))),
            jnp.pad(b_in[OFF_GNSA:OFF_GMERGE], (0, 128 - ngate))),
        gm=seg(OFF_GMERGE, w_in.shape[1]),
        w1k=bf(p["w_cmp_k1"]), w1v=bf(p["w_cmp_v1"]), pek=pe_rows(p["pe_cmp_k"]), pev=pe_rows(p["pe_cmp_v"]),
        w2k=bf(p["w_cmp_k2"]), w2v=bf(p["w_cmp_v2"]),
        w_br_nsa=bf(p["w_br_nsa"]), w_br_sb=bf(p["w_br_sb"]), w_br_mem=bf(p["w_br_mem"]), w_o=bf(p["w_o"]),
        ln1=(p["ln1_g"], p["ln1_b"]), ln2=(p["ln2_g"], p["ln2_b"]),
        up_a=(bf(p["w_up"][:, :d_ff]), p["b_up"][:d_ff], p["w_conv"][:, :d_ff], p["b_conv"][:d_ff]),
        up_g=(bf(p["w_up"][:, d_ff:]), p["b_up"][d_ff:], p["w_conv"][:, d_ff:], p["b_conv"][d_ff:]),
        w_down=bf(p["w_down"]), b_down=p["b_down"], d_ff=d_ff)


def _project(xb, w, tm):
    mm = functools.partial(_matmul, xb, tm=tm)
    (q,) = mm(*w["q"], [BF16])
    kv, kvb = mm(*w["kv"], [F32, BF16])
    win, winb = mm(*w["win"], [F32, BF16])
    (qsm,) = mm(*w["qsm"], [BF16])
    (gn,) = mm(*w["gn"], [F32], act="sigmoid")
    (gm,) = mm(*w["gm"], [F32], act="sigmoid")
    return q, kv, kvb, win, winb, qsm, gn, gm


def _post_mixer(x2d, o_nsa, o_sb, o_mem, gm, w, alpha, tm):
    merged = _merge(o_nsa, o_sb, o_mem, gm, w["w_br_nsa"], w["w_br_sb"], w["w_br_mem"], tm=tm)
    d = x2d.shape[1]
    return _matmul_ln(merged, w["w_o"], jnp.zeros((d,), F32), x2d, *w["ln1"], alpha, [F32, BF16], tm=min(tm, 512))


def _prompt_layer(x, mem_kv_b, w, alpha):
    B, T, D = x.shape
    M = B * T
    x2d = x.reshape(M, D)
    q, kv, kvb, win, winb, qsm, gn, gm = _project(x2d.astype(BF16), w, 1024)
    pages = kv.reshape(M // QB, QB, KV_CH)
    table = jnp.arange(M // QB, dtype=jnp.int32).reshape(B, T // QB)
    kcmp, vcmp = _compress(pages, table, w["w1k"], w["w1v"], w["pek"], w["pev"], w["w2k"], w["w2v"])
    slopes = jnp.repeat(_alibi_slopes(NSA_HEADS).reshape(NSA_GROUPS, NSA_HPG), QB, axis=1)[..., None]
    o_nsa = _nsa_seq(q.reshape(B, T, -1), kcmp, vcmp, kvb.reshape(B, T, -1), winb.reshape(B, T, -1),
                     gn.reshape(B, T, -1), slopes)
    o_sb, o_mem = _sbmem_seq(qsm.reshape(B, T, -1), kvb.reshape(B, T, -1), mem_kv_b)
    x1, x1b = _post_mixer(x2d, o_nsa.reshape(M, -1), o_sb.reshape(M, -1), o_mem.reshape(M, -1), gm, w, alpha, 1024)
    d_ff = w["d_ff"]
    zero_h = jnp.zeros((B, CONV_W - 1, d_ff), F32)
    act, st_a, st_g = _ffn_up_seq(x1b.reshape(B, T, D), w["up_a"][0], w["up_g"][0], w["up_a"][1], w["up_g"][1],
                                  w["up_a"][2], w["up_g"][2], w["up_a"][3], w["up_g"][3], zero_h, zero_h)
    (y,) = _matmul_ln(act.reshape(M, d_ff), w["w_down"], w["b_down"], x1, *w["ln2"], alpha, [F32])
    win3 = win.reshape(B, T, WIN_CH)
    keep = min(WINDOW, T)
    win_out = win3[:, T - keep:] if T >= WINDOW else jnp.concatenate(
        [jnp.zeros((B, WINDOW, WIN_CH), F32), win3], axis=1)[:, -keep:]
    return (y.reshape(B, T, D), kv.reshape(B, T, KV_CH), win_out, jnp.concatenate([st_a, st_g], axis=-1))


def _sample_layer(x, mem_kv, pages, table, win_buf, conv_buf, w, alpha):
    B, T, D = x.shape
    M = B * T
    P = table.shape[1]
    pos0 = P * QB
    n_hist = win_buf.shape[1]
    assert T <= TPAD and T >= 2
    x2d = x.reshape(M, D)
    q, kv, _, win, _, qsm, gn, gm = _project(x2d.astype(BF16), w, M)
    kcmp, vcmp = _compress(pages, table, w["w1k"], w["w1v"], w["pek"], w["pev"], w["w2k"], w["w2v"])
    n_blk = -(-(pos0 + T) // SEL_BLOCK)
    padt = lambda a, ax: jnp.pad(a, [(0, TPAD - T) if i == ax else (0, 0) for i in range(a.ndim)])

    q5 = padt(q.reshape(B, T, NSA_GROUPS, NSA_HPG, HEAD_DIM).transpose(0, 2, 3, 1, 4), 3)
    q_rows = q5.reshape(B, NSA_GROUPS, NSA_HPG * TPAD, HEAD_DIM)
    eye_h = jnp.eye(SB_HEADS, dtype=BF16)

    def block_diag(qh):
        qh = padt(qh.reshape(B, T, SB_HEADS, HEAD_DIM).transpose(0, 2, 1, 3), 2)
        return (qh[:, :, :, None, :] * eye_h[None, :, None, :, None]).reshape(B, SB_HEADS * TPAD, SB_W)

    qsb_bd, qm_bd = block_diag(qsm[:, :SB_W]), block_diag(qsm[:, SB_W:])
    slopes = _alibi_slopes(NSA_HEADS).reshape(NSA_GROUPS, NSA_HPG)
    slope_rows = jnp.repeat(slopes, TPAD, axis=1)[..., None]

    win3 = win.reshape(B, T, WIN_CH)
    o_cmp, o_win, o_mem, sel = _dec_local(q_rows, kcmp, vcmp, win_buf, padt(win3, 1), qm_bd, mem_kv, slope_rows,
                                          pos0, T, n_blk)

    new_page = padt(kv.reshape(B, T, KV_CH), 1)
    o_slc = _dec_slc(pages, table, new_page, q_rows, sel[..., :n_blk], slopes, pos0)
    o_sb = _dec_sb(pages, table, new_page, qsb_bd, pos0, T)

    rows_of = lambda o: o.reshape(B, -1, TPAD, HEAD_DIM)[:, :, :T].transpose(0, 2, 1, 3).reshape(M, -1)
    gexp = jnp.repeat(gn[:, :3 * NSA_HEADS].reshape(M, 3, NSA_HEADS), HEAD_DIM, axis=-1)
    o_nsa = _gate3([gexp[:, br] for br in range(3)], rows_of(o_cmp), rows_of(o_slc), rows_of(o_win))

    x1, x1b = _post_mixer(x2d, o_nsa, rows_of(o_sb).astype(BF16), rows_of(o_mem).astype(BF16), gm, w, alpha, M)
    d_ff = w["d_ff"]
    hist = lambda back: jnp.pad(conv_buf[:, CONV_W - 1 - back:], ((0, 0), (0, T - back), (0, 0))).reshape(M, 2 * d_ff)
    h1, h2 = hist(1), hist(2)
    act, u_a, u_g = _ffn_up_rows(x1b, w["up_a"][0], w["up_g"][0], w["up_a"][1], w["up_g"][1], w["up_a"][2],
                                 w["up_g"][2], w["up_a"][3], w["up_g"][3],
                                 h1[:, :d_ff], h2[:, :d_ff], h1[:, d_ff:], h2[:, d_ff:], T)
    (y,) = _matmul_ln(act, w["w_down"], w["b_down"], x1, *w["ln2"], alpha, [F32], tm=M)
    u = jnp.concatenate([u_a, u_g], axis=-1).reshape(B, T, 2 * d_ff)
    win_out = jnp.concatenate([win_buf, win3], axis=1)[:, -n_hist:] if T > n_hist else \
        jnp.concatenate([win_buf[:, T:], win3], axis=1)
    return (y.reshape(B, T, D), kv.reshape(B, T, KV_CH), win_out, u[:, T - (CONV_W - 1):])


def kernel(x_prompt, x_sample, mem_prompt, cache_kv_pages, page_table, cache_win_kv, cache_mem_kv, state_ffn_conv, w_in, b_in, w_cmp_k1, pe_cmp_k, w_cmp_k2, w_cmp_v1, pe_cmp_v, w_cmp_v2, w_br_nsa, w_br_sb, w_br_mem, w_o, w_mem_kv, b_mem_kv, ln1_g, ln1_b, w_up, b_up, w_conv, b_conv, w_down, b_down, ln2_g, ln2_b):
    depth = w_in.shape[0]
    alpha = (2.0 * depth) ** 0.25
    params = dict(w_in=w_in, b_in=b_in, w_cmp_k1=w_cmp_k1, pe_cmp_k=pe_cmp_k, w_cmp_k2=w_cmp_k2, w_cmp_v1=w_cmp_v1,
                  pe_cmp_v=pe_cmp_v, w_cmp_v2=w_cmp_v2, w_br_nsa=w_br_nsa, w_br_sb=w_br_sb, w_br_mem=w_br_mem,
                  w_o=w_o, ln1_g=ln1_g, ln1_b=ln1_b, w_up=w_up, b_up=b_up, w_conv=w_conv, b_conv=b_conv,
                  w_down=w_down, b_down=b_down, ln2_g=ln2_g, ln2_b=ln2_b)
    hp, hs = x_prompt, x_sample
    outs = [[] for _ in range(7)]
    B, Mt, D = mem_prompt.shape
    at = lambda a, l: a.reshape(a.shape[1:]) if a.shape[0] == 1 else a[l]
    for l in range(depth):
        w = _layer_weights({k: at(v, l) for k, v in params.items()})
        mem_kv, mem_kv_b = _matmul(mem_prompt.reshape(B * Mt, D).astype(BF16), at(w_mem_kv, l).astype(BF16),
                                   at(b_mem_kv, l), [F32, BF16], tm=B * Mt)
        hp, kv_p, win_p, conv_p = _prompt_layer(hp, mem_kv_b.reshape(B, Mt, -1), w, alpha)
        hs, kv_s, win_s, conv_s = _sample_layer(hs, at(cache_mem_kv, l), at(cache_kv_pages, l), page_table,
                                                at(cache_win_kv, l), at(state_ffn_conv, l), w, alpha)
        for o, v in zip(outs, (kv_p, win_p, mem_kv.reshape(B, Mt, -1), conv_p, kv_s, win_s, conv_s)):
            o.append(v)
    return (hp, hs) + tuple(jnp.stack(o) for o in outs)
```

```python
import functools
import math

import jax
import jax.numpy as jnp
from jax import lax
from jax.experimental import pallas as pl
from jax.experimental.pallas import tpu as pltpu

F32 = jnp.float32
BF16 = jnp.bfloat16
NEG = -1e30

HEAD_DIM = 128
NSA_HEADS = 8
NSA_GROUPS = 2
NSA_HPG = NSA_HEADS // NSA_GROUPS
CMP_LEN = 32
CMP_STRIDE = 16
SEL_BLOCK = 64
N_SEL = 16
WINDOW = 512
SB_HEADS = 4
MEM_HEADS = 4
CONV_W = 3
QB = 128
TPAD = 8
LN_EPS = 1e-5
PAGES_PER_STEP = 16
SB_CUT = 88.0
VMEM_LIMIT = 56 * 1024 * 1024

NSA_Q = NSA_HEADS * HEAD_DIM
KVG = NSA_GROUPS * HEAD_DIM
SB_W = SB_HEADS * HEAD_DIM
MEM_W = MEM_HEADS * HEAD_DIM
KV_CH = 4 * KVG + 2 * SB_W
WIN_CH = 2 * KVG
OFF_KV = NSA_Q
OFF_WIN = OFF_KV + KV_CH
OFF_QSB = OFF_WIN + WIN_CH
OFF_GNSA = OFF_QSB + SB_W + MEM_W
OFF_GMERGE = OFF_GNSA + 3 * NSA_HEADS
SCALE = HEAD_DIM ** -0.5


def _cparams(*sem):
    return pltpu.CompilerParams(dimension_semantics=sem, vmem_limit_bytes=VMEM_LIMIT)


def _nt(a, b):
    return lax.dot_general(a, b, (((1,), (1,)), ((), ())), preferred_element_type=F32)


def _dot(a, b):
    return jnp.dot(a, b, preferred_element_type=F32)


def _iota(shape, dim):
    return lax.broadcasted_iota(jnp.int32, shape, dim)


def _softmax_rows(s, mask):
    s = jnp.where(mask, s, NEG)
    m = jnp.max(s, axis=-1, keepdims=True)
    e = jnp.where(mask, jnp.exp(s - m), 0.0)
    return e / jnp.maximum(jnp.sum(e, axis=-1, keepdims=True), 1e-30)


def _split3(x):
    hi = x.astype(BF16)
    r1 = x - hi.astype(F32)
    mid = r1.astype(BF16)
    lo = (r1 - mid.astype(F32)).astype(BF16)
    return hi, mid, lo


def _softplus_pair(z):
    lg = jnp.log1p(jnp.exp(-jnp.abs(z)))
    return jnp.maximum(z, 0.0) + lg, jnp.minimum(z, 0.0) - lg


def _rank_select(score, lane, n_blk, n_top):
    cnt = jnp.zeros(score.shape, jnp.int32)
    for j in range(n_blk):
        col = score[:, j:j + 1]
        beats = (col > score) | ((col == score) & (lane > j))
        cnt = cnt + jnp.where(beats, 1, 0)
    return jnp.where((cnt < n_top) & (score >= 0.0), 1.0, 0.0)


def _rank_select_t(score_t, n_blk, n_top):
    blk = _iota((score_t.shape[0], 1), 0)
    cnt = jnp.zeros(score_t.shape, jnp.int32)
    for j in range(n_blk):
        row = score_t[j:j + 1, :]
        beats = (row > score_t) | ((row == score_t) & (blk > j))
        cnt = cnt + jnp.where(beats, 1, 0)
    return jnp.where((cnt < n_top) & (score_t >= 0.0), 1.0, 0.0)


def _pages_per_step(n_pages):
    return next(n for n in range(PAGES_PER_STEP, 0, -1) if n_pages % n == 0)


def _page_rows(new, cols):
    x = new[:, cols].astype(BF16)
    return jnp.concatenate([x, jnp.zeros((QB - x.shape[0], x.shape[1]), BF16)], axis=0)


def _block_scores(imp, pos, lane, n_blk):
    cur = pos // SEL_BLOCK
    forced = (lane == 0) | (lane == cur) | (lane == cur - 1)
    visible = lane * SEL_BLOCK <= pos
    score = jnp.where(forced, 1e9, jnp.where(visible, imp, -1.0))
    return jnp.where(lane < n_blk, score, -2.0)


def _overlap(nsub, lanes, n_cmp, n_blk, transposed=False):
    shape = (lanes, nsub) if transposed else (nsub, lanes)
    n = _iota(shape, 1 if transposed else 0)
    j = _iota(shape, 0 if transposed else 1)
    st = n * CMP_STRIDE
    ov = (st < (j + 1) * SEL_BLOCK) & (st + CMP_LEN > j * SEL_BLOCK) & (n < n_cmp) & (j < n_blk)
    return jnp.where(ov, 1.0, 0.0).astype(BF16)


def _mm_body(x_ref, w_ref, b_ref, *o_refs, act):
    r = _dot(x_ref[...], w_ref[...]) + b_ref[...]
    if act == "sigmoid":
        r = jax.nn.sigmoid(r)
    for o in o_refs:
        o[...] = r.astype(o.dtype)


def _matmul(x, w, b, out_dtypes, act=None, tm=1024, tn=512):
    M, K = x.shape
    N = w.shape[1]
    tm, tn = min(tm, M), min(tn, N)
    assert M % tm == 0 and N % tn == 0
    return pl.pallas_call(
        functools.partial(_mm_body, act=act),
        grid=(M // tm, N // tn),
        in_specs=[pl.BlockSpec((tm, K), lambda i, j: (i, 0)),
                  pl.BlockSpec((K, tn), lambda i, j: (0, j)),
                  pl.BlockSpec((1, tn), lambda i, j: (0, j))],
        out_specs=[pl.BlockSpec((tm, tn), lambda i, j: (i, j)) for _ in out_dtypes],
        out_shape=[jax.ShapeDtypeStruct((M, N), d) for d in out_dtypes],
        compiler_params=_cparams("parallel", "parallel"),
    )(x, w, b.reshape(1, N).astype(F32))


def _mm_ln_body(x_ref, w_ref, b_ref, res_ref, g_ref, be_ref, *rest, nk, alpha, n_out):
    o_refs, acc_ref = rest[:n_out], rest[n_out]
    k = pl.program_id(1)

    part = _dot(x_ref[...], w_ref[...])

    @pl.when(k == 0)
    def _():
        acc_ref[...] = part

    @pl.when(k > 0)
    def _():
        acc_ref[...] += part

    @pl.when(k == nk - 1)
    def _():
        v = alpha * res_ref[...] + (acc_ref[...] + b_ref[...])
        mu = jnp.mean(v, axis=-1, keepdims=True)
        d = v - mu
        var = jnp.mean(d * d, axis=-1, keepdims=True)
        y = d * lax.rsqrt(var + LN_EPS) * g_ref[...] + be_ref[...]
        for o in o_refs:
            o[...] = y.astype(o.dtype)


def _matmul_ln(x, w, b, res, g, be, alpha, out_dtypes, tm=512, tk_max=1536):
    M, K = x.shape
    N = w.shape[1]
    tm = min(tm, M)
    tk = max(t for t in range(128, min(tk_max, K) + 1, 128) if K % t == 0)
    assert M % tm == 0 and K % tk == 0
    nk = K // tk
    row = lambda a: a.reshape(1, N).astype(F32)
    return pl.pallas_call(
        functools.partial(_mm_ln_body, nk=nk, alpha=alpha, n_out=len(out_dtypes)),
        grid=(M // tm, nk),
        in_specs=[pl.BlockSpec((tm, tk), lambda i, k: (i, k)),
                  pl.BlockSpec((tk, N), lambda i, k: (k, 0)),
                  pl.BlockSpec((1, N), lambda i, k: (0, 0)),
                  pl.BlockSpec((tm, N), lambda i, k: (i, 0)),
                  pl.BlockSpec((1, N), lambda i, k: (0, 0)),
                  pl.BlockSpec((1, N), lambda i, k: (0, 0))],
        out_specs=[pl.BlockSpec((tm, N), lambda i, k: (i, 0)) for _ in out_dtypes],
        out_shape=[jax.ShapeDtypeStruct((M, N), d) for d in out_dtypes],
        scratch_shapes=[pltpu.VMEM((tm, N), F32)],
        compiler_params=_cparams("parallel", "arbitrary"),
    )(x, w, row(b), res, row(g), row(be))


def _merge_body(on_ref, os_ref, om_ref, g0_ref, g1_ref, g2_ref, wn_ref, ws_ref, wm_ref, o_ref):
    r = (g0_ref[...] * _dot(on_ref[...], wn_ref[...])
         + g1_ref[...] * _dot(os_ref[...], ws_ref[...])
         + g2_ref[...] * _dot(om_ref[...], wm_ref[...]))
    o_ref[...] = r.astype(o_ref.dtype)


def _merge(o_nsa, o_sb, o_mem, gm, w_nsa, w_sb, w_mem, tm=1024, tn=512):
    M = o_nsa.shape[0]
    D = w_nsa.shape[1]
    tm, tn = min(tm, M), min(tn, D)
    nj = D // tn
    xs = lambda a: pl.BlockSpec((tm, a.shape[1]), lambda i, j: (i, 0))
    ws = lambda a: pl.BlockSpec((a.shape[0], tn), lambda i, j: (0, j))
    gs = lambda c: pl.BlockSpec((tm, tn), lambda i, j: (i, c * nj + j))
    return pl.pallas_call(
        _merge_body,
        grid=(M // tm, nj),
        in_specs=[xs(o_nsa), xs(o_sb), xs(o_mem), gs(0), gs(1), gs(2), ws(w_nsa), ws(w_sb), ws(w_mem)],
        out_specs=pl.BlockSpec((tm, tn), lambda i, j: (i, j)),
        out_shape=jax.ShapeDtypeStruct((M, D), BF16),
        compiler_params=_cparams("parallel", "parallel"),
    )(o_nsa, o_sb, o_mem, gm, gm, gm, w_nsa, w_sb, w_mem)


def _conv_taps(u, p1, p2, wc_ref, bc_ref):
    return bc_ref[...] + wc_ref[0:1, :] * p2 + wc_ref[1:2, :] * p1 + wc_ref[2:3, :] * u


def _ffn_up_seq_body(x_ref, wa_ref, wg_ref, ba_ref, bg_ref, wca_ref, wcg_ref, bca_ref, bcg_ref, ha_ref, hg_ref,
                     act_ref, sa_ref, sg_ref, ca_ref, cg_ref, *, nt, tm):
    t = pl.program_id(2)

    @pl.when(t == 0)
    def _():
        ca_ref[0:2, :] = ha_ref[0]
        cg_ref[0:2, :] = hg_ref[0]

    x = x_ref[0]
    r = _iota((tm, 1), 0)

    def half(w_ref, b_ref, wc_ref, bc_ref, c_ref, s_ref):
        u = _dot(x, w_ref[...]) + b_ref[...]
        h0, h1 = c_ref[0:1, :], c_ref[1:2, :]
        p1 = jnp.where(r == 0, h1, pltpu.roll(u, 1, 0))
        p2 = jnp.where(r == 0, h0, jnp.where(r == 1, h1, pltpu.roll(u, 2, 0)))
        c = _conv_taps(u, p1, p2, wc_ref, bc_ref)
        c_ref[0:2, :] = u[tm - 2:tm, :]

        @pl.when(t == nt - 1)
        def _():
            s_ref[0] = u[tm - 2:tm, :]

        return c

    a = half(wa_ref, ba_ref, wca_ref, bca_ref, ca_ref, sa_ref)
    g = half(wg_ref, bg_ref, wcg_ref, bcg_ref, cg_ref, sg_ref)
    act_ref[0] = (a * jax.nn.gelu(g, approximate=True)).astype(act_ref.dtype)


def _ffn_up_seq(x, w_a, w_g, b_a, b_g, wc_a, wc_g, bc_a, bc_g, h_a, h_g, tm=1024, tn=512):
    B, T, D = x.shape
    Fh = w_a.shape[1]
    tm, tn = min(tm, T), min(tn, Fh)
    assert T % tm == 0 and Fh % tn == 0 and tm >= 8
    nt = T // tm
    wsp = pl.BlockSpec((D, tn), lambda j, b, t: (0, j))
    rsp = pl.BlockSpec((1, tn), lambda j, b, t: (0, j))
    csp = pl.BlockSpec((CONV_W, tn), lambda j, b, t: (0, j))
    hsp = pl.BlockSpec((1, 2, tn), lambda j, b, t: (b, 0, j))
    row = lambda a: a.reshape(1, Fh)
    return pl.pallas_call(
        functools.partial(_ffn_up_seq_body, nt=nt, tm=tm),
        grid=(Fh // tn, B, nt),
        in_specs=[pl.BlockSpec((1, tm, D), lambda j, b, t: (b, t, 0)), wsp, wsp, rsp, rsp, csp, csp, rsp, rsp, hsp, hsp],
        out_specs=[pl.BlockSpec((1, tm, tn), lambda j, b, t: (b, t, j)), hsp, hsp],
        out_shape=[jax.ShapeDtypeStruct((B, T, Fh), BF16),
                   jax.ShapeDtypeStruct((B, 2, Fh), F32), jax.ShapeDtypeStruct((B, 2, Fh), F32)],
        scratch_shapes=[pltpu.VMEM((8, tn), F32), pltpu.VMEM((8, tn), F32)],
        compiler_params=_cparams("parallel", "parallel", "arbitrary"),
    )(x, w_a, w_g, row(b_a), row(b_g), wc_a, wc_g, row(bc_a), row(bc_g), h_a, h_g)


def _ffn_up_rows_body(x_ref, wa_ref, wg_ref, ba_ref, bg_ref, wca_ref, wcg_ref, bca_ref, bcg_ref,
                      h1a_ref, h2a_ref, h1g_ref, h2g_ref, act_ref, ua_ref, ug_ref, *, period):
    x = x_ref[...]
    tin = _iota((x.shape[0], 1), 0) % period

    def half(w_ref, b_ref, wc_ref, bc_ref, h1_ref, h2_ref, u_ref):
        u = _dot(x, w_ref[...]) + b_ref[...]
        u_ref[...] = u
        p1 = jnp.where(tin >= 1, pltpu.roll(u, 1, 0), h1_ref[...])
        p2 = jnp.where(tin >= 2, pltpu.roll(u, 2, 0), h2_ref[...])
        return _conv_taps(u, p1, p2, wc_ref, bc_ref)

    a = half(wa_ref, ba_ref, wca_ref, bca_ref, h1a_ref, h2a_ref, ua_ref)
    g = half(wg_ref, bg_ref, wcg_ref, bcg_ref, h1g_ref, h2g_ref, ug_ref)
    act_ref[...] = (a * jax.nn.gelu(g, approximate=True)).astype(act_ref.dtype)


def _ffn_up_rows(x, w_a, w_g, b_a, b_g, wc_a, wc_g, bc_a, bc_g, h1a, h2a, h1g, h2g, period, tn=512):
    M, D = x.shape
    Fh = w_a.shape[1]
    tn = min(tn, Fh)
    wsp = pl.BlockSpec((D, tn), lambda j: (0, j))
    rsp = pl.BlockSpec((1, tn), lambda j: (0, j))
    csp = pl.BlockSpec((CONV_W, tn), lambda j: (0, j))
    msp = pl.BlockSpec((M, tn), lambda j: (0, j))
    row = lambda a: a.reshape(1, Fh)
    return pl.pallas_call(
        functools.partial(_ffn_up_rows_body, period=period),
        grid=(Fh // tn,),
        in_specs=[pl.BlockSpec((M, D), lambda j: (0, 0)), wsp, wsp, rsp, rsp, csp, csp, rsp, rsp, msp, msp, msp, msp],
        out_specs=[msp, msp, msp],
        out_shape=[jax.ShapeDtypeStruct((M, Fh), BF16), jax.ShapeDtypeStruct((M, Fh), F32),
                   jax.ShapeDtypeStruct((M, Fh), F32)],
        compiler_params=_cparams("parallel"),
    )(x, w_a, w_g, row(b_a), row(b_g), wc_a, wc_g, row(bc_a), row(bc_g), h1a, h2a, h1g, h2g)


def _compress_body(pt_ref, *refs, n_steps, nsub, npg):
    page_refs = refs[:npg]
    (w1k_ref, w1v_ref, pek_ref, pev_ref, wpk_ref, wpv_ref, perm_ref, w2k_ref, w2v_ref,
     ok_ref, ov_ref, xs_ref) = refs[npg:]
    st = pl.program_id(1)
    sub = QB // CMP_STRIDE
    for jp in range(npg // 2):
        x2 = jnp.concatenate([page_refs[2 * jp][0], page_refs[2 * jp + 1][0]], axis=0).astype(BF16)
        xp = _dot(perm_ref[...], x2).astype(BF16)
        row0 = pl.multiple_of((st * npg + 2 * jp) * sub, 2 * sub)
        for s in range(CMP_STRIDE):
            xs_ref[s, pl.ds(row0, 2 * sub), :] = xp[s * 2 * sub:(s + 1) * 2 * sub, :]

    @pl.when(st == n_steps - 1)
    def _():
        for kv, (w1_ref, pe_ref, wp_ref, w2_ref, o_ref) in enumerate(
                ((w1k_ref, pek_ref, wpk_ref, w2k_ref, ok_ref), (w1v_ref, pev_ref, wpv_ref, w2v_ref, ov_ref))):
            acc = jnp.zeros((NSA_GROUPS * nsub, 2 * HEAD_DIM), F32)
            for s in range(0, CMP_STRIDE, 2):
                xg = jnp.concatenate(
                    [jnp.concatenate([xs_ref[s + d, :, kv * KVG + g * HEAD_DIM:kv * KVG + (g + 1) * HEAD_DIM]
                                      for d in range(2)], axis=1) for g in range(NSA_GROUPS)], axis=0)
                acc = acc + _dot(xg, w1_ref[s // 2])
            pe_term = _dot(pe_ref[...], wp_ref[...])[0:1, :]
            hid = acc[:, :HEAD_DIM] + pltpu.roll(acc[:, HEAD_DIM:], NSA_GROUPS * nsub - 1, 0) + pe_term
            out = _dot(jax.nn.gelu(hid, approximate=True).astype(BF16), w2_ref[...])
            for g in range(NSA_GROUPS):
                o_ref[0, g] = out[g * nsub:(g + 1) * nsub, :].astype(o_ref.dtype)


def _compress(pages, table, w1k, w1v, pek, pev, w2k, w2v):
    B, P = table.shape
    nsub = P * (QB // CMP_STRIDE)
    assert CMP_LEN == 2 * CMP_STRIDE
    npg = _pages_per_step(P)
    assert npg % 2 == 0
    sub = QB // CMP_STRIDE

    o = jnp.arange(2 * QB)
    src = ((o % (2 * sub)) // sub) * QB + CMP_STRIDE * (o % sub) + o // (2 * sub)
    perm = (src[:, None] == o[None, :]).astype(BF16)

    def pair_weights(w1):
        w = w1.reshape(2, CMP_STRIDE // 2, 2, HEAD_DIM, HEAD_DIM).transpose(1, 2, 3, 0, 4)
        return w.reshape(CMP_STRIDE // 2, 2 * HEAD_DIM, 2 * HEAD_DIM)

    flat_pe = lambda pe: jnp.broadcast_to(pe.reshape(1, CMP_LEN * HEAD_DIM), (8, CMP_LEN * HEAD_DIM)).astype(BF16)
    flat_w = lambda w1: w1.reshape(CMP_LEN * HEAD_DIM, HEAD_DIM)

    full = lambda a: pl.BlockSpec(a.shape, lambda b, s, pt: (0,) * a.ndim)
    osp = pl.BlockSpec((1, NSA_GROUPS, nsub, HEAD_DIM), lambda b, s, pt: (b, 0, 0, 0))
    osh = jax.ShapeDtypeStruct((B, NSA_GROUPS, nsub, HEAD_DIM), BF16)
    page = lambda j: pl.BlockSpec((1, QB, 2 * KVG), lambda b, s, pt: (pt[b, s * npg + j], 0, 0))
    consts = (pair_weights(w1k), pair_weights(w1v), flat_pe(pek), flat_pe(pev), flat_w(w1k), flat_w(w1v), perm,
              w2k, w2v)
    return pl.pallas_call(
        functools.partial(_compress_body, n_steps=P // npg, nsub=nsub, npg=npg),
        grid_spec=pltpu.PrefetchScalarGridSpec(
            num_scalar_prefetch=1, grid=(B, P // npg),
            in_specs=[page(j) for j in range(npg)] + [full(a) for a in consts],
            out_specs=[osp, osp],
            scratch_shapes=[pltpu.VMEM((CMP_STRIDE, nsub, 2 * KVG), BF16)]),
        out_shape=[osh, osh],
        compiler_params=_cparams("parallel", "arbitrary"),
    )(table, *([pages] * npg), *consts)


def _nsa_seq_body(q_ref, kc_ref, vc_ref, ks_ref, vs_ref, kw_ref, vw_ref, gate_ref, slope_ref, hslope_ref, o_ref,
                  s_ref, mx_ref, sum_ref, acc_ref, *, n_cmp, n_blk, n_top, unroll_slc, unroll_win):
    g = pl.program_id(1)
    i = pl.program_id(2)
    t0 = i * QB
    R = NSA_HPG * QB
    q = q_ref[0]
    qs = jnp.concatenate([q[:, h * HEAD_DIM:(h + 1) * HEAD_DIM] for h in range(NSA_HPG)], axis=0)
    rowt = _iota((R, 1), 0) % QB
    pos = t0 + rowt
    slope = slope_ref[0]
    lane = _iota((1, QB), 1)

    kc = kc_ref[0, 0]
    nsub = kc.shape[0]
    ncol = _iota((1, nsub), 1)
    dist_c = pos - (ncol * CMP_STRIDE + (CMP_LEN - 1))
    s = _nt(qs, kc) * SCALE - slope * dist_c.astype(F32)
    p_c = _softmax_rows(s, (dist_c >= 0) & (ncol < n_cmp)).astype(BF16)
    o_cmp = _dot(p_c, vc_ref[0, 0])
    imp4 = _nt(_overlap(nsub, QB, n_cmp, n_blk, transposed=True), p_c)
    imp_t = imp4[:, 0:QB]
    for h in range(1, NSA_HPG):
        imp_t = imp_t + imp4[:, h * QB:(h + 1) * QB]
    blk = _iota((QB, 1), 0)
    posq = t0 + lane
    cur = posq // SEL_BLOCK
    forced = (blk == 0) | (blk == cur) | (blk == cur - 1)
    score_t = jnp.where(forced, 1e9, jnp.where(blk * SEL_BLOCK <= posq, imp_t, -1.0))
    score_t = jnp.where(blk < n_blk, score_t, -2.0)
    nb = 8 * (-(-n_blk // 8))
    sel_t = _rank_select_t(score_t[0:nb], n_blk, n_top)
    if nb < QB:
        sel_t = jnp.concatenate([sel_t, jnp.zeros((QB - nb, QB), F32)], axis=0)
    sel = sel_t.T.astype(BF16)

    rk = (_iota((QB, 1), 0) - lane).astype(F32)

    def attend(lo, hi, k_ref, v_ref, mask_fn, unroll):
        mx_ref[...] = jnp.full(mx_ref.shape, NEG, F32)
        g_lo, g_hi = lo // unroll, (hi + unroll - 1) // unroll

        def scores(cg, carry):
            for u in range(unroll):
                c = cg * unroll + u
                k0 = pl.multiple_of(c * QB, QB)
                dist = rk + (t0 - k0).astype(F32)
                ok = mask_fn(c, dist)
                qk = _nt(qs, k_ref[0, pl.ds(k0, QB), :])
                for h in range(NSA_HPG):
                    rows = slice(h * QB, (h + 1) * QB)
                    sc = jnp.where(ok, qk[rows] * SCALE - hslope_ref[g * NSA_HPG + h] * dist, NEG)
                    s_ref[rows, pl.ds(k0, QB)] = sc
                    mx_ref[rows, :] = jnp.maximum(mx_ref[rows, :], sc)
            return carry

        lax.fori_loop(g_lo, g_hi, scores, 0)
        m = jnp.max(mx_ref[...], axis=-1, keepdims=True)
        mx_ref[...] = jnp.broadcast_to(m, (R, QB))
        sum_ref[...] = jnp.zeros_like(sum_ref)
        acc_ref[...] = jnp.zeros_like(acc_ref)

        def weigh(cg, carry):
            for u in range(unroll):
                k0 = pl.multiple_of((cg * unroll + u) * QB, QB)
                p = jnp.exp(s_ref[:, pl.ds(k0, QB)] - mx_ref[...])
                sum_ref[...] += p
                acc_ref[...] += _dot(p.astype(BF16), v_ref[0, pl.ds(k0, QB), :])
            return carry

        lax.fori_loop(g_lo, g_hi, weigh, 0)
        l = jnp.sum(sum_ref[...], axis=-1, keepdims=True)
        return acc_ref[...] / jnp.maximum(l, 1e-30)

    def slc_mask(c, dist):
        j = _iota((QB, QB), 0)
        k = _iota((QB, QB), 1)
        e = jnp.where(j == 2 * c + k // SEL_BLOCK, 1.0, 0.0).astype(BF16)
        return (_dot(sel, e) > 0.5) & (dist >= 0.0)

    o_slc = attend(0, i + 1, ks_ref, vs_ref, slc_mask, unroll_slc)
    o_win = attend(jnp.maximum(i - WINDOW // QB, 0), i + 1, kw_ref, vw_ref,
                   lambda c, dist: (dist >= 0.0) & (dist < float(WINDOW)), unroll_win)

    gate = gate_ref[0]
    outs = []
    for h in range(NSA_HPG):
        rows = slice(h * QB, (h + 1) * QB)
        o_h = jnp.zeros((QB, HEAD_DIM), F32)
        for br, o_br in enumerate((o_cmp, o_slc, o_win)):
            col = br * NSA_HEADS + g * NSA_HPG + h
            gcol = jnp.sum(jnp.where(lane == col, gate, 0.0), axis=-1, keepdims=True)
            o_h = o_h + gcol * o_br[rows]
        outs.append(o_h)
    o_ref[0] = jnp.concatenate(outs, axis=-1).astype(o_ref.dtype)


def _nsa_seq(q, kcmp, vcmp, kvb, winb, gates, slopes):
    B, T, _ = q.shape
    nq = T // QB
    nsub = kcmp.shape[2]
    n_blk = -(-T // SEL_BLOCK)
    assert T % QB == 0 and n_blk <= QB and WINDOW % QB == 0
    divisor = lambda n: next(u for u in range(n, 0, -1) if nq % u == 0)
    R = NSA_HPG * QB
    csp = pl.BlockSpec((1, 1, nsub, HEAD_DIM), lambda b, g, i: (b, g, 0, 0))
    col = lambda c0: pl.BlockSpec((1, T, HEAD_DIM), lambda b, g, i: (b, 0, c0 + g))
    tile = pltpu.VMEM((R, QB), F32)
    return pl.pallas_call(
        functools.partial(_nsa_seq_body, n_cmp=nsub - 1, n_blk=n_blk, n_top=min(N_SEL, n_blk),
                          unroll_slc=divisor(4), unroll_win=divisor(2)),
        grid=(B, NSA_GROUPS, nq),
        in_specs=[pl.BlockSpec((1, QB, NSA_HPG * HEAD_DIM), lambda b, g, i: (b, i, g)), csp, csp,
                  col(2 * NSA_GROUPS), col(3 * NSA_GROUPS), col(0), col(NSA_GROUPS),
                  pl.BlockSpec((1, QB, 128), lambda b, g, i: (b, i, 0)),
                  pl.BlockSpec((1, R, 1), lambda b, g, i: (g, 0, 0)),
                  pl.BlockSpec(memory_space=pltpu.SMEM)],
        out_specs=pl.BlockSpec((1, QB, NSA_HPG * HEAD_DIM), lambda b, g, i: (b, i, g)),
        out_shape=jax.ShapeDtypeStruct((B, T, NSA_Q), BF16),
        scratch_shapes=[pltpu.VMEM((R, T), F32), tile, tile, tile],
        compiler_params=_cparams("parallel", "parallel", "arbitrary"),
    )(q, kcmp, vcmp, kvb, kvb, winb, winb, gates, slopes, slopes[:, ::QB, 0].reshape(-1))


def _sb_chunk(z, before, r, upper):
    sp, ls = _softplus_pair(z)
    nlk = jnp.where(before, sp, 0.0)
    hi, mid, lo = _split3(nlk)
    later = _dot(hi, upper) + _dot(mid, upper) + _dot(lo, upper)
    a = jnp.where(before, jnp.exp(ls - (r + later)), 0.0)
    return a, jnp.sum(nlk, axis=-1, keepdims=True)


def _upper(n):
    return jnp.where(_iota((n, n), 0) > _iota((n, n), 1), 1.0, 0.0).astype(BF16)


def _sbmem_seq_body(qs_ref, qm_ref, k_ref, v_ref, km_ref, vm_ref, osb_ref, om_ref, r_ref, acc_ref):
    i = pl.program_id(1)
    t0 = i * QB
    pos = t0 + _iota((QB, 1), 0)
    lane = _iota((1, QB), 1)
    upper = _upper(QB)
    r_ref[...] = jnp.zeros_like(r_ref)
    acc_ref[...] = jnp.zeros_like(acc_ref)

    def body(carry):
        step, _ = carry
        k0 = pl.multiple_of((i - step) * QB, QB)
        before = (pos - (k0 + lane)) > 0
        heads = [slice(h * HEAD_DIM, (h + 1) * HEAD_DIM) for h in range(SB_HEADS)]
        pairs = [_softplus_pair(_nt(qs_ref[0, :, c], k_ref[0, pl.ds(k0, QB), c]) * SCALE) for c in heads]
        nlk = [jnp.where(before, sp, 0.0) for sp, _ in pairs]
        later = _dot(jnp.concatenate([t for x in nlk for t in _split3(x)], axis=0), upper)
        r_new = []
        for h, c in enumerate(heads):
            rows = later[3 * h * QB:(3 * h + 3) * QB]
            between = r_ref[h] + rows[0:QB] + rows[QB:2 * QB] + rows[2 * QB:3 * QB]
            a = jnp.where(before, jnp.exp(pairs[h][1] - between), 0.0)
            acc_ref[h] += _dot(a.astype(BF16), v_ref[0, pl.ds(k0, QB), c])
            r_new.append(r_ref[h] + jnp.sum(nlk[h], axis=-1, keepdims=True))
            r_ref[h] = r_new[h]
        rmin = jnp.min(functools.reduce(jnp.minimum, r_new))
        return step + 1, rmin

    lax.while_loop(lambda c: (c[0] <= i) & (c[1] < SB_CUT), body, (jnp.int32(0), jnp.float32(0.0)))
    for h in range(SB_HEADS):
        cols = slice(h * HEAD_DIM, (h + 1) * HEAD_DIM)
        osb_ref[0, :, cols] = acc_ref[h].astype(osb_ref.dtype)
        s = _nt(qm_ref[0, :, cols], km_ref[0, :, cols]) * SCALE
        p = _softmax_rows(s, jnp.full(s.shape, True))
        om_ref[0, :, cols] = _dot(p.astype(BF16), vm_ref[0, :, cols]).astype(om_ref.dtype)


def _sbmem_seq(qsm, kvb, memb):
    B, T, _ = qsm.shape
    Mt = memb.shape[1]
    nq = T // QB
    assert SB_W == MEM_W and (4 * KVG) % SB_W == 0
    qsp = lambda c: pl.BlockSpec((1, QB, SB_W), lambda b, i: (b, i, c))
    ksp = lambda c: pl.BlockSpec((1, T, SB_W), lambda b, i: (b, 0, c))
    msp = lambda c: pl.BlockSpec((1, Mt, MEM_W), lambda b, i: (b, 0, c))
    osp = pl.BlockSpec((1, QB, SB_W), lambda b, i: (b, i, 0))
    sb0 = 4 * KVG // SB_W
    return pl.pallas_call(
        _sbmem_seq_body,
        grid=(B, nq),
        in_specs=[qsp(0), qsp(1), ksp(sb0), ksp(sb0 + 1), msp(0), msp(1)],
        out_specs=[osp, osp],
        out_shape=[jax.ShapeDtypeStruct((B, T, SB_W), BF16), jax.ShapeDtypeStruct((B, T, MEM_W), BF16)],
        scratch_shapes=[pltpu.VMEM((SB_HEADS, QB, 1), F32), pltpu.VMEM((SB_HEADS, QB, HEAD_DIM), F32)],
        compiler_params=_cparams("parallel", "arbitrary"),
    )(qsm, qsm, kvb, kvb, memb, memb)


def _dec_local_body(q_ref, kc_ref, vc_ref, win_ref, wnew_ref, qm_ref, mem_ref, slope_ref, ocmp_ref, owin_ref, omem_ref,
                    sel_ref, *, n_cmp, n_blk, n_top, pos0, n_new):
    R = NSA_HPG * TPAD
    row = _iota((R, 1), 0)
    pos = pos0 + row % TPAD
    nsub = kc_ref.shape[2]
    lanes = sel_ref.shape[3]
    lane = _iota((1, lanes), 1)
    ncol = _iota((1, nsub), 1)
    n_hist = win_ref.shape[1]
    kidx = _iota((1, n_hist + QB), 1)
    kp = pos0 - n_hist + kidx
    ov = _overlap(nsub, lanes, n_cmp, n_blk)
    for g in range(NSA_GROUPS):
        q = q_ref[0, g]
        slope = slope_ref[g]
        dist_c = pos - (ncol * CMP_STRIDE + (CMP_LEN - 1))
        s = _nt(q, kc_ref[0, g]) * SCALE - slope * dist_c.astype(F32)
        p_c = _softmax_rows(s, (dist_c >= 0) & (ncol < n_cmp)).astype(BF16)
        ocmp_ref[0, g] = _dot(p_c, vc_ref[0, g])
        imp4 = _dot(p_c, ov)
        imp = imp4[0:TPAD]
        for h in range(1, NSA_HPG):
            imp = imp + imp4[h * TPAD:(h + 1) * TPAD]
        score = _block_scores(imp, pos0 + _iota((TPAD, 1), 0), lane, n_blk)
        sel_ref[0, g] = _rank_select(score, lane, n_blk, n_top)

        kcols = slice(g * HEAD_DIM, (g + 1) * HEAD_DIM)
        vcols = slice(KVG + g * HEAD_DIM, KVG + (g + 1) * HEAD_DIM)
        kw = jnp.concatenate([win_ref[0, :, kcols].astype(BF16), _page_rows(wnew_ref[0], kcols)], axis=0)
        vw = jnp.concatenate([win_ref[0, :, vcols].astype(BF16), _page_rows(wnew_ref[0], vcols)], axis=0)
        dist_w = pos - kp
        s = _nt(q, kw) * SCALE - slope * dist_w.astype(F32)
        ok = (kidx < n_hist + n_new) & (kp >= 0) & (dist_w >= 0) & (dist_w < WINDOW)
        owin_ref[0, g] = _dot(_softmax_rows(s, ok).astype(BF16), vw)

    km = mem_ref[0, :, :MEM_W].astype(BF16)
    vm = mem_ref[0, :, MEM_W:].astype(BF16)
    s = _nt(qm_ref[0], km) * SCALE
    o_all = _dot(_softmax_rows(s, jnp.full(s.shape, True)).astype(BF16), vm)
    hrow = _iota((MEM_HEADS * TPAD, 1), 0) // TPAD
    o = jnp.zeros((MEM_HEADS * TPAD, HEAD_DIM), F32)
    for h in range(MEM_HEADS):
        o = o + jnp.where(hrow == h, o_all[:, h * HEAD_DIM:(h + 1) * HEAD_DIM], 0.0)
    omem_ref[0] = o


def _dec_local(q, kcmp, vcmp, win_buf, win_new, qm_bd, mem, slopes, pos0, n_new, n_blk):
    B = q.shape[0]
    R = NSA_HPG * TPAD
    nsub = kcmp.shape[2]
    lanes = 128 * (-(-n_blk // 128))
    b4 = lambda a: pl.BlockSpec((1,) + a.shape[1:], lambda b: (b,) + (0,) * (a.ndim - 1))
    osh = jax.ShapeDtypeStruct((B, NSA_GROUPS, R, HEAD_DIM), F32)
    osp = pl.BlockSpec((1, NSA_GROUPS, R, HEAD_DIM), lambda b: (b, 0, 0, 0))
    return pl.pallas_call(
        functools.partial(_dec_local_body, n_cmp=nsub - 1, n_blk=n_blk, n_top=min(N_SEL, n_blk), pos0=pos0,
                          n_new=n_new),
        grid=(B,),
        in_specs=[b4(q), b4(kcmp), b4(vcmp), b4(win_buf), b4(win_new), b4(qm_bd), b4(mem),
                  pl.BlockSpec(slopes.shape, lambda b: (0, 0, 0))],
        out_specs=[osp, osp, pl.BlockSpec((1, R, HEAD_DIM), lambda b: (b, 0, 0)),
                   pl.BlockSpec((1, NSA_GROUPS, TPAD, lanes), lambda b: (b, 0, 0, 0))],
        out_shape=[osh, osh, jax.ShapeDtypeStruct((B, R, HEAD_DIM), F32),
                   jax.ShapeDtypeStruct((B, NSA_GROUPS, TPAD, lanes), F32)],
        compiler_params=_cparams("parallel"),
    )(q, kcmp, vcmp, win_buf, win_new, qm_bd, mem, slopes)


def _dec_slc_body(pt_ref, *refs, npg, n_steps, pos0):
    page_refs = refs[:npg]
    new_ref, q_ref, sel_ref, seln_ref, e_ref, slope_ref, o_ref, m_ref, l_ref, acc_ref = refs[npg:]
    st = pl.program_id(1)
    R = NSA_HEADS * TPAD
    pos = pos0 + _iota((R, 1), 0) % TPAD
    slope = slope_ref[...]
    q = q_ref[0]
    slc0 = 2 * KVG

    def update(k, v, k0, chosen):
        dist = pos - (k0 + _iota((1, k.shape[0]), 1))
        mask = chosen & (dist >= 0)
        sc = jnp.where(mask, _nt(q, k) * SCALE - slope * dist.astype(F32), NEG)
        m_old = m_ref[...]
        m_new = jnp.maximum(m_old, jnp.max(sc, axis=-1, keepdims=True))
        a = jnp.exp(m_old - m_new)
        p = jnp.where(mask, jnp.exp(sc - m_new), 0.0)
        l_ref[...] = a * l_ref[...] + jnp.sum(p, axis=-1, keepdims=True)
        acc_ref[...] = a * acc_ref[...] + _dot(p.astype(BF16), v)
        m_ref[...] = m_new

    @pl.when(st == 0)
    def _():
        m_ref[...] = jnp.full(m_ref.shape, NEG, F32)
        l_ref[...] = jnp.zeros_like(l_ref)
        acc_ref[...] = jnp.zeros_like(acc_ref)
        new = new_ref[0]
        update(_page_rows(new, slice(slc0, slc0 + KVG)), _page_rows(new, slice(slc0 + KVG, slc0 + 2 * KVG)), pos0,
               seln_ref[0, :, 0:1] > 0.5)

    k = jnp.concatenate([p[0, :, 0:KVG].astype(BF16) for p in page_refs], axis=0)
    v = jnp.concatenate([p[0, :, KVG:2 * KVG].astype(BF16) for p in page_refs], axis=0)
    update(k, v, st * (npg * QB), _dot(sel_ref[0, 0], e_ref[...]) > 0.5)

    @pl.when(st == n_steps - 1)
    def _():
        o = acc_ref[...] / jnp.maximum(l_ref[...], 1e-30)
        rows = NSA_HPG * TPAD
        for g in range(NSA_GROUPS):
            o_ref[0, g] = o[g * rows:(g + 1) * rows, g * HEAD_DIM:(g + 1) * HEAD_DIM]


def _dec_slc(pages, table, new_page, q_rows, sel, slopes, pos0):
    B, P = table.shape
    rows = NSA_HPG * TPAD
    R = NSA_GROUPS * rows
    npg = _pages_per_step(P)
    n_steps = P // npg
    bps = npg * (QB // SEL_BLOCK)
    assert bps <= 128
    eye_g = jnp.eye(NSA_GROUPS, dtype=BF16)
    q_bd = (q_rows[:, :, :, None, :] * eye_g[None, :, None, :, None]).reshape(B, R, KVG)
    sel_rows = jnp.broadcast_to(sel[:, :, None], (B, NSA_GROUPS, NSA_HPG) + sel.shape[2:]).reshape(B, R, -1)
    lanes = lambda a: jnp.pad(a, [(0, 0)] * (a.ndim - 1) + [(0, 128 - a.shape[-1])]).astype(BF16)
    sel_steps = lanes(sel_rows[:, :, :n_steps * bps].reshape(B, R, n_steps, bps).transpose(0, 2, 1, 3))
    n_blk = sel.shape[-1]
    sel_new = lanes(sel_rows[:, :, n_steps * bps:n_steps * bps + 1]) if n_blk > n_steps * bps \
        else jnp.zeros((B, R, 128), BF16)
    expand = (_iota((128, npg * QB), 0) == _iota((128, npg * QB), 1) // SEL_BLOCK).astype(BF16)
    slope_rows = jnp.repeat(slopes.reshape(-1), TPAD)[:, None]

    b4 = lambda a: pl.BlockSpec((1,) + a.shape[1:], lambda b, s, pt: (b,) + (0,) * (a.ndim - 1))
    full = lambda a: pl.BlockSpec(a.shape, lambda b, s, pt: (0,) * a.ndim)
    page = lambda j: pl.BlockSpec((1, QB, 2 * KVG), lambda b, s, pt: (pt[b, s * npg + j], 0, 1))
    return pl.pallas_call(
        functools.partial(_dec_slc_body, npg=npg, n_steps=n_steps, pos0=pos0),
        grid_spec=pltpu.PrefetchScalarGridSpec(
            num_scalar_prefetch=1, grid=(B, n_steps),
            in_specs=[page(j) for j in range(npg)]
            + [b4(new_page), b4(q_bd), pl.BlockSpec((1, 1, R, 128), lambda b, s, pt: (b, s, 0, 0)), b4(sel_new),
               full(expand), full(slope_rows)],
            out_specs=pl.BlockSpec((1, NSA_GROUPS, rows, HEAD_DIM), lambda b, s, pt: (b, 0, 0, 0)),
            scratch_shapes=[pltpu.VMEM((R, 1), F32), pltpu.VMEM((R, 1), F32), pltpu.VMEM((R, KVG), F32)]),
        out_shape=jax.ShapeDtypeStruct((B, NSA_GROUPS, rows, HEAD_DIM), F32),
        compiler_params=_cparams("parallel", "arbitrary"),
    )(table, *([pages] * npg), new_page, q_bd, sel_steps, sel_new, expand, slope_rows)


def _dec_sb_body(pt_ref, pages_ref, new_ref, q_ref, o_ref, buf_ref, sem_ref, r_ref, acc_ref, *, n_pages, pos0, n_new):
    b = pl.program_id(0)
    R = SB_HEADS * TPAD
    row = _iota((R, 1), 0)
    pos = pos0 + row % TPAD
    live = row % TPAD < n_new
    lane = _iota((1, QB), 1)
    upper = _upper(QB)
    sb0 = 4 * KVG

    def page_copy(page, slot):
        return pltpu.make_async_copy(pages_ref.at[pt_ref[b, page], :, pl.ds(sb0, 2 * SB_W)], buf_ref.at[slot],
                                     sem_ref.at[slot])

    page_copy(n_pages - 1, 0).start()
    r_ref[...] = jnp.zeros_like(r_ref)
    acc_ref[...] = jnp.zeros_like(acc_ref)

    def chunk(k, v, k0):
        before = (pos - (k0 + lane)) > 0
        z = _nt(q_ref[0], k.astype(BF16)) * SCALE
        a, mass = _sb_chunk(z, before, r_ref[...], upper)
        acc_ref[...] += _dot(a.astype(BF16), v.astype(BF16))
        r_new = r_ref[...] + mass
        r_ref[...] = r_new
        return jnp.min(jnp.where(live, r_new, SB_CUT))

    new = new_ref[0]
    rmin0 = chunk(_page_rows(new, slice(sb0, sb0 + SB_W)), _page_rows(new, slice(sb0 + SB_W, sb0 + 2 * SB_W)), pos0)

    def body(carry):
        j, _ = carry
        slot = j % 2
        page = n_pages - 1 - j
        page_copy(page, slot).wait()

        @pl.when(j + 1 < n_pages)
        def _():
            page_copy(page - 1, 1 - slot).start()

        rmin = chunk(buf_ref[slot, :, 0:SB_W], buf_ref[slot, :, SB_W:2 * SB_W], page * QB)
        return j + 1, rmin

    done, _ = lax.while_loop(lambda c: (c[0] < n_pages) & (c[1] < SB_CUT), body, (jnp.int32(0), rmin0))

    @pl.when(done < n_pages)
    def _():
        page_copy(n_pages - 1 - done, done % 2).wait()

    hrow = row // TPAD
    o = jnp.zeros((R, HEAD_DIM), F32)
    for h in range(SB_HEADS):
        o = o + jnp.where(hrow == h, acc_ref[:, h * HEAD_DIM:(h + 1) * HEAD_DIM], 0.0)
    o_ref[0] = o


def _dec_sb(pages, table, new_page, q_bd, pos0, n_new):
    B, P = table.shape
    R = SB_HEADS * TPAD
    return pl.pallas_call(
        functools.partial(_dec_sb_body, n_pages=P, pos0=pos0, n_new=n_new),
        grid_spec=pltpu.PrefetchScalarGridSpec(
            num_scalar_prefetch=1, grid=(B,),
            in_specs=[pl.BlockSpec(memory_space=pl.ANY),
                      pl.BlockSpec((1, TPAD, KV_CH), lambda b, pt: (b, 0, 0)),
                      pl.BlockSpec((1, R, SB_W), lambda b, pt: (b, 0, 0))],
            out_specs=pl.BlockSpec((1, R, HEAD_DIM), lambda b, pt: (b, 0, 0)),
            scratch_shapes=[pltpu.VMEM((2, QB, 2 * SB_W), F32), pltpu.SemaphoreType.DMA((2,)),
                            pltpu.VMEM((R, 1), F32), pltpu.VMEM((R, SB_W), F32)]),
        out_shape=jax.ShapeDtypeStruct((B, R, HEAD_DIM), F32),
        compiler_params=_cparams("arbitrary"),
    )(table, pages, new_page, q_bd)


def _gate3_body(ga_ref, gb_ref, gc_ref, a_ref, b_ref, c_ref, o_ref):
    o_ref[...] = (ga_ref[...] * a_ref[...] + gb_ref[...] * b_ref[...] + gc_ref[...] * c_ref[...]).astype(o_ref.dtype)


def _gate3(gates, a, b, c):
    return pl.pallas_call(_gate3_body, out_shape=jax.ShapeDtypeStruct(a.shape, BF16))(*gates, a, b, c)


def _alibi_slopes(n):
    return jnp.exp2(-8.0 * jnp.arange(1, n + 1, dtype=F32) / n)


def _layer_weights(p):
    bf = lambda a: a.astype(BF16)
    w_in, b_in = p["w_in"], p["b_in"]
    seg = lambda a, b: (bf(w_in[:, a:b]), b_in[a:b])
    ngate = 3 * NSA_HEADS
    d_ff = p["w_down"].shape[0]
    pe_rows = lambda pe: pe.astype(F32)
    return dict(
        q=seg(0, OFF_KV), kv=seg(OFF_KV, OFF_WIN), win=seg(OFF_WIN, OFF_QSB), qsm=seg(OFF_QSB, OFF_GNSA),
        gn=(bf(jnp.pad(w_in[:, OFF_GNSA:OFF_GMERGE], ((0, 0), (0, 128 - ngate)))),
            jnp.pad(b_in[OFF_GNSA:OFF_GMERGE], (0, 128 - ngate))),
        gm=seg(OFF_GMERGE, w_in.shape[1]),
        w1k=bf(p["w_cmp_k1"]), w1v=bf(p["w_cmp_v1"]), pek=pe_rows(p["pe_cmp_k"]), pev=pe_rows(p["pe_cmp_v"]),
        w2k=bf(p["w_cmp_k2"]), w2v=bf(p["w_cmp_v2"]),
        w_br_nsa=bf(p["w_br_nsa"]), w_br_sb=bf(p["w_br_sb"]), w_br_mem=bf(p["w_br_mem"]), w_o=bf(p["w_o"]),
        ln1=(p["ln1_g"], p["ln1_b"]), ln2=(p["ln2_g"], p["ln2_b"]),
        up_a=(bf(p["w_up"][:, :d_ff]), p["b_up"][:d_ff], p["w_conv"][:, :d_ff], p["b_conv"][:d_ff]),
        up_g=(bf(p["w_up"][:, d_ff:]), p["b_up"][d_ff:], p["w_conv"][:, d_ff:], p["b_conv"][d_ff:]),
        w_down=bf(p["w_down"]), b_down=p["b_down"], d_ff=d_ff)


def _project(xb, w, tm):
    mm = functools.partial(_matmul, xb, tm=tm)
    (q,) = mm(*w["q"], [BF16])
    kv, kvb = mm(*w["kv"], [F32, BF16])
    win, winb = mm(*w["win"], [F32, BF16])
    (qsm,) = mm(*w["qsm"], [BF16])
    (gn,) = mm(*w["gn"], [F32], act="sigmoid")
    (gm,) = mm(*w["gm"], [F32], act="sigmoid")
    return q, kv, kvb, win, winb, qsm, gn, gm


def _post_mixer(x2d, o_nsa, o_sb, o_mem, gm, w, alpha, tm):
    merged = _merge(o_nsa, o_sb, o_mem, gm, w["w_br_nsa"], w["w_br_sb"], w["w_br_mem"], tm=tm)
    d = x2d.shape[1]
    return _matmul_ln(merged, w["w_o"], jnp.zeros((d,), F32), x2d, *w["ln1"], alpha, [F32, BF16], tm=min(tm, 512))


def _prompt_layer(x, mem_kv_b, w, alpha):
    B, T, D = x.shape
    M = B * T
    x2d = x.reshape(M, D)
    q, kv, kvb, win, winb, qsm, gn, gm = _project(x2d.astype(BF16), w, 1024)
    pages = kv.reshape(M // QB, QB, KV_CH)
    table = jnp.arange(M // QB, dtype=jnp.int32).reshape(B, T // QB)
    kcmp, vcmp = _compress(pages, table, w["w1k"], w["w1v"], w["pek"], w["pev"], w["w2k"], w["w2v"])
    slopes = jnp.repeat(_alibi_slopes(NSA_HEADS).reshape(NSA_GROUPS, NSA_HPG), QB, axis=1)[..., None]
    o_nsa = _nsa_seq(q.reshape(B, T, -1), kcmp, vcmp, kvb.reshape(B, T, -1), winb.reshape(B, T, -1),
                     gn.reshape(B, T, -1), slopes)
    o_sb, o_mem = _sbmem_seq(qsm.reshape(B, T, -1), kvb.reshape(B, T, -1), mem_kv_b)
    x1, x1b = _post_mixer(x2d, o_nsa.reshape(M, -1), o_sb.reshape(M, -1), o_mem.reshape(M, -1), gm, w, alpha, 1024)
    d_ff = w["d_ff"]
    zero_h = jnp.zeros((B, CONV_W - 1, d_ff), F32)
    act, st_a, st_g = _ffn_up_seq(x1b.reshape(B, T, D), w["up_a"][0], w["up_g"][0], w["up_a"][1], w["up_g"][1],
                                  w["up_a"][2], w["up_g"][2], w["up_a"][3], w["up_g"][3], zero_h, zero_h)
    (y,) = _matmul_ln(act.reshape(M, d_ff), w["w_down"], w["b_down"], x1, *w["ln2"], alpha, [F32])
    win3 = win.reshape(B, T, WIN_CH)
    keep = min(WINDOW, T)
    win_out = win3[:, T - keep:] if T >= WINDOW else jnp.concatenate(
        [jnp.zeros((B, WINDOW, WIN_CH), F32), win3], axis=1)[:, -keep:]
    return (y.reshape(B, T, D), kv.reshape(B, T, KV_CH), win_out, jnp.concatenate([st_a, st_g], axis=-1))


def _sample_layer(x, mem_kv, pages, table, win_buf, conv_buf, w, alpha):
    B, T, D = x.shape
    M = B * T
    P = table.shape[1]
    pos0 = P * QB
    n_hist = win_buf.shape[1]
    assert T <= TPAD and T >= 2
    x2d = x.reshape(M, D)
    q, kv, _, win, _, qsm, gn, gm = _project(x2d.astype(BF16), w, M)
    kcmp, vcmp = _compress(pages, table, w["w1k"], w["w1v"], w["pek"], w["pev"], w["w2k"], w["w2v"])
    n_blk = -(-(pos0 + T) // SEL_BLOCK)
    padt = lambda a, ax: jnp.pad(a, [(0, TPAD - T) if i == ax else (0, 0) for i in range(a.ndim)])

    q5 = padt(q.reshape(B, T, NSA_GROUPS, NSA_HPG, HEAD_DIM).transpose(0, 2, 3, 1, 4), 3)
    q_rows = q5.reshape(B, NSA_GROUPS, NSA_HPG * TPAD, HEAD_DIM)
    eye_h = jnp.eye(SB_HEADS, dtype=BF16)

    def block_diag(qh):
        qh = padt(qh.reshape(B, T, SB_HEADS, HEAD_DIM).transpose(0, 2, 1, 3), 2)
        return (qh[:, :, :, None, :] * eye_h[None, :, None, :, None]).reshape(B, SB_HEADS * TPAD, SB_W)

    qsb_bd, qm_bd = block_diag(qsm[:, :SB_W]), block_diag(qsm[:, SB_W:])
    slopes = _alibi_slopes(NSA_HEADS).reshape(NSA_GROUPS, NSA_HPG)
    slope_rows = jnp.repeat(slopes, TPAD, axis=1)[..., None]

    win3 = win.reshape(B, T, WIN_CH)
    o_cmp, o_win, o_mem, sel = _dec_local(q_rows, kcmp, vcmp, win_buf, padt(win3, 1), qm_bd, mem_kv, slope_rows,
                                          pos0, T, n_blk)

    new_page = padt(kv.reshape(B, T, KV_CH), 1)
    o_slc = _dec_slc(pages, table, new_page, q_rows, sel[..., :n_blk], slopes, pos0)
    o_sb = _dec_sb(pages, table, new_page, qsb_bd, pos0, T)

    rows_of = lambda o: o.reshape(B, -1, TPAD, HEAD_DIM)[:, :, :T].transpose(0, 2, 1, 3).reshape(M, -1)
    gexp = jnp.repeat(gn[:, :3 * NSA_HEADS].reshape(M, 3, NSA_HEADS), HEAD_DIM, axis=-1)
    o_nsa = _gate3([gexp[:, br] for br in range(3)], rows_of(o_cmp), rows_of(o_slc), rows_of(o_win))

    x1, x1b = _post_mixer(x2d, o_nsa, rows_of(o_sb).astype(BF16), rows_of(o_mem).astype(BF16), gm, w, alpha, M)
    d_ff = w["d_ff"]
    hist = lambda back: jnp.pad(conv_buf[:, CONV_W - 1 - back:], ((0, 0), (0, T - back), (0, 0))).reshape(M, 2 * d_ff)
    h1, h2 = hist(1), hist(2)
    act, u_a, u_g = _ffn_up_rows(x1b, w["up_a"][0], w["up_g"][0], w["up_a"][1], w["up_g"][1], w["up_a"][2],
                                 w["up_g"][2], w["up_a"][3], w["up_g"][3],
                                 h1[:, :d_ff], h2[:, :d_ff], h1[:, d_ff:], h2[:, d_ff:], T)
    (y,) = _matmul_ln(act, w["w_down"], w["b_down"], x1, *w["ln2"], alpha, [F32], tm=M)
    u = jnp.concatenate([u_a, u_g], axis=-1).reshape(B, T, 2 * d_ff)
    win_out = jnp.concatenate([win_buf, win3], axis=1)[:, -n_hist:] if T > n_hist else \
        jnp.concatenate([win_buf[:, T:], win3], axis=1)
    return (y.reshape(B, T, D), kv.reshape(B, T, KV_CH), win_out, u[:, T - (CONV_W - 1):])


def kernel(x_prompt, x_sample, mem_prompt, cache_kv_pages, page_table, cache_win_kv, cache_mem_kv, state_ffn_conv, w_in, b_in, w_cmp_k1, pe_cmp_k, w_cmp_k2, w_cmp_v1, pe_cmp_v, w_cmp_v2, w_br_nsa, w_br_sb, w_br_mem, w_o, w_mem_kv, b_mem_kv, ln1_g, ln1_b, w_up, b_up, w_conv, b_conv, w_down, b_down, ln2_g, ln2_b):
    depth = w_in.shape[0]
    alpha = (2.0 * depth) ** 0.25
    params = dict(w_in=w_in, b_in=b_in, w_cmp_k1=w_cmp_k1, pe_cmp_k=pe_cmp_k, w_cmp_k2=w_cmp_k2, w_cmp_v1=w_cmp_v1,
                  pe_cmp_v=pe_cmp_v, w_cmp_v2=w_cmp_v2, w_br_nsa=w_br_nsa, w_br_sb=w_br_sb, w_br_mem=w_br_mem,
                  w_o=w_o, ln1_g=ln1_g, ln1_b=ln1_b, w_up=w_up, b_up=b_up, w_conv=w_conv, b_conv=b_conv,
                  w_down=w_down, b_down=b_down, ln2_g=ln2_g, ln2_b=ln2_b)
    hp, hs = x_prompt, x_sample
    outs = [[] for _ in range(7)]
    B, Mt, D = mem_prompt.shape
    at = lambda a, l: a.reshape(a.shape[1:]) if a.shape[0] == 1 else a[l]
    for l in range(depth):
        w = _layer_weights({k: at(v, l) for k, v in params.items()})
        mem_kv, mem_kv_b = _matmul(mem_prompt.reshape(B * Mt, D).astype(BF16), at(w_mem_kv, l).astype(BF16),
                                   at(b_mem_kv, l), [F32, BF16], tm=B * Mt)
        hp, kv_p, win_p, conv_p = _prompt_layer(hp, mem_kv_b.reshape(B, Mt, -1), w, alpha)
        hs, kv_s, win_s, conv_s = _sample_layer(hs, at(cache_mem_kv, l), at(cache_kv_pages, l), page_table,
                                                at(cache_win_kv, l), at(state_ffn_conv, l), w, alpha)
        for o, v in zip(outs, (kv_p, win_p, mem_kv.reshape(B, Mt, -1), conv_p, kv_s, win_s, conv_s)):
            o.append(v)
    return (hp, hs) + tuple(jnp.stack(o) for o in outs)
```
